```python
import math
import jax, jax.numpy as jnp
from jax import lax
import numpy as np

D_MODEL = 4096
BATCH = 4
SEQ = 2048
DEPTH = 4
DEC_BATCH = 128
DEC_SEQ = 1
PAST_LEN = 8192
PAGE_SIZE = 128

HA = 16
D_NOPE = 128
D_ROPE = 64
D_QK_A = D_NOPE + D_ROPE
D_VA = 128
Q_LORA = 768
KV_LORA = 256
ROPE_THETA = 10000.0
HB = 32
HD_B = 64
H_IDX = 32
D_IDX = 64
DSA_TOPK = 256
N_BUCKETS = 32
MAX_DIST = 128
PEER_HEADS = 8
PEER_QDIM = 256
N_KEYS = 128
N_EXPERTS = N_KEYS * N_KEYS
PEER_TOPK = 16
PEER_CHUNK = 64
Q_BLOCK = 128
EPS = 1e-6

W_A = HA * D_VA
W_B = HB * HD_B
MLA_ROW = KV_LORA + D_ROPE
DSA_ROW = 2 * HD_B + D_IDX
IN_SPLITS = (Q_LORA, KV_LORA, D_ROPE, HB * HD_B, HD_B, HD_B, H_IDX * D_IDX, D_IDX, H_IDX, D_MODEL, D_MODEL)
C_IN = sum(IN_SPLITS)

kernel_name = "hybrid_mla_dsa_peer_decode_step"


def rmsnorm(x, g):
    xf = x.astype(jnp.float32)
    y = xf * lax.rsqrt(jnp.mean(xf * xf, axis=-1, keepdims=True) + EPS)
    return (y * g.astype(jnp.float32)).astype(x.dtype)


def rope(x, pos):
    half = x.shape[-1] // 2
    inv_freq = ROPE_THETA ** (-jnp.arange(half, dtype=jnp.float32) / half)
    ang = pos.astype(jnp.float32)[:, None] * inv_freq[None, :]
    ang = ang.reshape((1, pos.shape[0]) + (1,) * (x.ndim - 3) + (half,))
    cos, sin = jnp.cos(ang), jnp.sin(ang)
    xf = x.astype(jnp.float32)
    x1, x2 = xf[..., :half], xf[..., half:]
    return jnp.concatenate([x1 * cos - x2 * sin, x1 * sin + x2 * cos], axis=-1).astype(x.dtype)


def t5_bucket(dist):
    n = jnp.maximum(dist, 0)
    max_exact = N_BUCKETS // 2
    nf = jnp.maximum(n, 1).astype(jnp.float32)
    large = max_exact + (jnp.log(nf / max_exact) / math.log(MAX_DIST / max_exact)
                         * (N_BUCKETS - max_exact)).astype(jnp.int32)
    large = jnp.minimum(large, N_BUCKETS - 1)
    return jnp.where(n < max_exact, n, large)


def sweep_query_blocks(fn, q_arrays, q_pos):
    B, T = q_arrays[0].shape[:2]
    tb = min(Q_BLOCK, T)
    nb = -(-T // tb)
    pad = nb * tb - T

    def split(a):
        a = jnp.pad(a, [(0, 0), (0, pad)] + [(0, 0)] * (a.ndim - 2))
        return jnp.moveaxis(a.reshape((B, nb, tb) + a.shape[2:]), 1, 0)

    pos = jnp.pad(q_pos, (0, pad), mode="edge").reshape(nb, tb)
    out = lax.map(lambda args: fn(*args), tuple(split(a) for a in q_arrays) + (pos,))
    out = jnp.moveaxis(out, 0, 1)
    return out.reshape((B, nb * tb) + out.shape[3:])[:, :T]


def mla_attention(q_lat, q_pe, q_pos, c_kv, k_pe, k_inv, k_pos):
    scale = D_QK_A ** -0.5
    kinv_t = jnp.swapaxes(k_inv, 1, 2).astype(jnp.float32)[:, :, None, :]

    def block(ql, qp, pos):
        s = (jnp.einsum("bthr,bsr->bhts", ql, c_kv)
             + jnp.einsum("bthp,bsp->bhts", qp, k_pe)).astype(jnp.float32)
        s = s * kinv_t * scale
        s = jnp.where(k_pos[None, None, None, :] <= pos[None, None, :, None], s, -jnp.inf)
        p = jax.nn.softmax(s, axis=-1).astype(c_kv.dtype)
        return jnp.einsum("bhts,bsr->bthr", p, c_kv)

    return sweep_query_blocks(block, (q_lat, q_pe), q_pos)


def dsa_attention(q, q_idx, w_idx, q_pos, k, v, k_idx, k_pos, rel_bias):
    S = k.shape[1]
    topk = max(1, min(DSA_TOPK, S // 4))
    scale = HD_B ** -0.5
    take = jax.vmap(lambda a, i: a[i])

    def block(qb, qib, wb, pos):
        sc = jnp.einsum("bthd,bsd->bths", qib, k_idx).astype(jnp.float32)
        sc = jnp.einsum("bths,bth->bts", jax.nn.relu(sc), wb.astype(jnp.float32))
        sc = jnp.where(k_pos[None, None, :] <= pos[None, :, None], sc, -jnp.inf)
        sel, idx = lax.top_k(sc, topk)
        valid = sel > -jnp.inf
        ks, vs = take(k, idx), take(v, idx)
        bucket = t5_bucket(pos[None, :, None] - k_pos[idx])
        bias = jnp.moveaxis(rel_bias[bucket], -1, 1).astype(jnp.float32)
        lg = jnp.einsum("bthd,btkd->bhtk", qb, ks).astype(jnp.float32) * scale + bias
        lg = jnp.where(valid[:, None], lg, -jnp.inf)
        p = jax.nn.softmax(lg, axis=-1).astype(v.dtype)
        return jnp.einsum("bhtk,btkd->bthd", p, vs)

    return sweep_query_blocks(block, (q, q_idx, w_idx), q_pos)


def peer_ffn(xn, wq, k1, k2, u, v):
    B, T, D = xn.shape
    N = B * T
    xt = xn.reshape(N, D)
    q = (xt @ wq).reshape(N, PEER_HEADS, 2, PEER_QDIM // 2)
    s1 = jnp.einsum("nhd,kd->nhk", q[:, :, 0], k1).astype(jnp.float32)
    s2 = jnp.einsum("nhd,kd->nhk", q[:, :, 1], k2).astype(jnp.float32)
    v1, i1 = lax.top_k(s1, PEER_TOPK)
    v2, i2 = lax.top_k(s2, PEER_TOPK)
    cand = (v1[..., :, None] + v2[..., None, :]).reshape(N, PEER_HEADS, PEER_TOPK * PEER_TOPK)
    cid = (i1[..., :, None] * N_KEYS + i2[..., None, :]).reshape(N, PEER_HEADS, PEER_TOPK * PEER_TOPK)
    best, pos = lax.top_k(cand, PEER_TOPK)
    eid = jnp.take_along_axis(cid, pos, axis=-1).reshape(N, PEER_HEADS * PEER_TOPK)
    gate = jax.nn.softmax(best, axis=-1).reshape(N, PEER_HEADS * PEER_TOPK).astype(xn.dtype)
    cs = min(PEER_CHUNK, N)
    nc = -(-N // cs)
    pad = nc * cs - N
    xc = jnp.pad(xt, ((0, pad), (0, 0))).reshape(nc, cs, D)
    ec = jnp.pad(eid, ((0, pad), (0, 0))).reshape(nc, cs, PEER_HEADS * PEER_TOPK)
    gc = jnp.pad(gate, ((0, pad), (0, 0))).reshape(nc, cs, PEER_HEADS * PEER_TOPK)

    def chunk(args):
        xb, eb, gb = args
        h = jnp.einsum("cd,ced->ce", xb, u[eb])
        a = gb * jax.nn.gelu(h, approximate=False)
        return jnp.einsum("ce,ced->cd", a, v[eb])

    out = lax.map(chunk, (xc, ec, gc)).reshape(nc * cs, D)[:N]
    return out.reshape(B, T, D)


def trunk_layer(x, q_pos, past, rel_bias, norm1_g, w_in, cq_norm_g, ckv_norm_g, w_uq, w_uk, w_uv,
                q_norm_a, k_norm_a, q_norm_b, k_norm_b, w_up_a, w_up_b, w_out, norm2_g,
                peer_wq, peer_k1, peer_k2, peer_u, peer_v):
    B, T, _ = x.shape
    dt = x.dtype
    xn = rmsnorm(x, norm1_g)
    z = xn @ w_in
    cuts = np.cumsum(IN_SPLITS)[:-1].tolist()
    z_cq, z_ckv, z_kpe, z_qb, z_kb, z_vb, z_qi, z_ki, z_wi, z_ga, z_gb = jnp.split(z, cuts, axis=-1)

    c_q = rmsnorm(z_cq, cq_norm_g)
    c_kv = rmsnorm(z_ckv, ckv_norm_g)
    q_a = rmsnorm((c_q @ w_uq).reshape(B, T, HA, D_QK_A), q_norm_a)
    q_pe = rope(q_a[..., D_NOPE:], q_pos)
    q_lat = jnp.einsum("bthd,rhd->bthr", q_a[..., :D_NOPE] * k_norm_a[:D_NOPE].astype(dt), w_uk)
    k_nope = jnp.einsum("btr,rhd->bthd", c_kv, w_uk).astype(jnp.float32)
    kpe_raw = z_kpe.astype(jnp.float32)
    k_sq = jnp.sum(k_nope * k_nope, axis=-1) + jnp.sum(kpe_raw * kpe_raw, axis=-1, keepdims=True)
    k_inv = lax.rsqrt(k_sq / D_QK_A + EPS).astype(dt)
    k_pe = rope(z_kpe * k_norm_a[D_NOPE:].astype(dt), q_pos)
    mla_row = jnp.concatenate([c_kv, k_pe], axis=-1)

    q_b = rmsnorm(z_qb.reshape(B, T, HB, HD_B), q_norm_b)
    k_b = rmsnorm(z_kb, k_norm_b)
    q_i = z_qi.reshape(B, T, H_IDX, D_IDX)
    w_i = z_wi * (H_IDX * D_IDX) ** -0.5
    dsa_row = jnp.concatenate([k_b, z_vb, z_ki], axis=-1)

    if past is None:
        mla_keys, kinv_keys, dsa_keys = mla_row, k_inv, dsa_row
    else:
        p_mla, p_kinv, p_dsa = past
        mla_keys = jnp.concatenate([p_mla, mla_row], axis=1)
        kinv_keys = jnp.concatenate([p_kinv, k_inv], axis=1)
        dsa_keys = jnp.concatenate([p_dsa, dsa_row], axis=1)
    k_pos = jnp.arange(mla_keys.shape[1], dtype=jnp.int32)

    ctx = mla_attention(q_lat, q_pe, q_pos, mla_keys[..., :KV_LORA], mla_keys[..., KV_LORA:], kinv_keys, k_pos)
    o_a = jnp.einsum("bthr,rhd->bthd", ctx, w_uv).reshape(B, T, W_A)
    o_b = dsa_attention(q_b, q_i, w_i, q_pos, dsa_keys[..., :HD_B], dsa_keys[..., HD_B:2 * HD_B],
                        dsa_keys[..., 2 * HD_B:], k_pos, rel_bias).reshape(B, T, W_B)

    g_a = jax.nn.sigmoid(z_ga.astype(jnp.float32)).astype(dt)
    g_b = jax.nn.sigmoid(z_gb.astype(jnp.float32)).astype(dt)
    mix = g_a * (o_a @ w_up_a) + g_b * (o_b @ w_up_b)
    h = x + mix @ w_out
    h = h + peer_ffn(rmsnorm(h, norm2_g), peer_wq, peer_k1, peer_k2, peer_u, peer_v)
    return h, mla_row, k_inv, dsa_row


def gather_pages(cache, layer, page_table):
    g = cache[layer, page_table]
    return g.reshape(page_table.shape[0], page_table.shape[1] * cache.shape[2], cache.shape[3])


def setup_inputs(seed: int = 0) -> dict:
    key = jax.random.key(seed)
    keys = jax.random.split(key, 40)
    kit = iter(range(40))
    f32 = jnp.float32

    def nrm(shape, scale):
        return jax.random.normal(keys[next(kit)], shape, f32) * scale

    def gain(shape):
        return 1.0 + 0.05 * jax.random.normal(keys[next(kit)], shape, f32)

    n_pages = PAST_LEN // PAGE_SIZE
    n_used = DEC_BATCH * n_pages
    n_pool = n_used + max(1, n_used // 4)
    perm = jax.random.permutation(keys[next(kit)], n_pool)
    page_table = perm[:n_used].reshape(DEC_BATCH, n_pages).astype(jnp.int32)

    inputs = {
        "x_prompt": nrm((BATCH, SEQ, D_MODEL), 1.0),
        "x_sample": nrm((DEC_BATCH, DEC_SEQ, D_MODEL), 1.0),
        "cache_mla": nrm((DEPTH, n_pool, PAGE_SIZE, MLA_ROW), 1.0),
        "cache_mla_kinv": jnp.exp(nrm((DEPTH, n_pool, PAGE_SIZE, HA), 0.1)),
        "cache_dsa": nrm((DEPTH, n_pool, PAGE_SIZE, DSA_ROW), 1.0),
        "page_table": page_table,
        "rel_bias": nrm((N_BUCKETS, HB), 0.5),
        "norm1_g": gain((DEPTH, D_MODEL)),
        "w_in": nrm((DEPTH, D_MODEL, C_IN), D_MODEL ** -0.5),
        "cq_norm_g": gain((DEPTH, Q_LORA)),
        "ckv_norm_g": gain((DEPTH, KV_LORA)),
        "w_uq": nrm((DEPTH, Q_LORA, HA * D_QK_A), Q_LORA ** -0.5),
        "w_uk": nrm((DEPTH, KV_LORA, HA, D_NOPE), KV_LORA ** -0.5),
        "w_uv": nrm((DEPTH, KV_LORA, HA, D_VA), KV_LORA ** -0.5),
        "q_norm_a": gain((DEPTH, D_QK_A)),
        "k_norm_a": gain((DEPTH, D_QK_A)),
        "q_norm_b": gain((DEPTH, HD_B)),
        "k_norm_b": gain((DEPTH, HD_B)),
        "w_up_a": nrm((DEPTH, W_A, D_MODEL), W_A ** -0.5),
        "w_up_b": nrm((DEPTH, W_B, D_MODEL), W_B ** -0.5),
        "w_out": nrm((DEPTH, D_MODEL, D_MODEL), D_MODEL ** -0.5),
        "norm2_g": gain((DEPTH, D_MODEL)),
        "peer_wq": nrm((DEPTH, D_MODEL, PEER_HEADS * PEER_QDIM), D_MODEL ** -0.5),
        "peer_k1": nrm((DEPTH, N_KEYS, PEER_QDIM // 2), (PEER_QDIM // 2) ** -0.5),
        "peer_k2": nrm((DEPTH, N_KEYS, PEER_QDIM // 2), (PEER_QDIM // 2) ** -0.5),
        "peer_u": nrm((DEPTH, N_EXPERTS, D_MODEL), D_MODEL ** -0.5),
        "peer_v": nrm((DEPTH, N_EXPERTS, D_MODEL), (PEER_HEADS * PEER_TOPK) ** -0.5),
    }
    return inputs


def reference(x_prompt, x_sample, cache_mla, cache_mla_kinv, cache_dsa, page_table, rel_bias,
              norm1_g, w_in, cq_norm_g, ckv_norm_g, w_uq, w_uk, w_uv, q_norm_a, k_norm_a,
              q_norm_b, k_norm_b, w_up_a, w_up_b, w_out, norm2_g,
              peer_wq, peer_k1, peer_k2, peer_u, peer_v):
    past_len = page_table.shape[1] * PAGE_SIZE
    pos_p = jnp.arange(x_prompt.shape[1], dtype=jnp.int32)
    pos_s = past_len + jnp.arange(x_sample.shape[1], dtype=jnp.int32)
    hp, hs = x_prompt, x_sample
    mla_p, kinv_p, dsa_p, mla_s, kinv_s, dsa_s = [], [], [], [], [], []
    for l in range(DEPTH):
        lw = (norm1_g[l], w_in[l], cq_norm_g[l], ckv_norm_g[l], w_uq[l], w_uk[l], w_uv[l],
              q_norm_a[l], k_norm_a[l], q_norm_b[l], k_norm_b[l], w_up_a[l], w_up_b[l], w_out[l],
              norm2_g[l], peer_wq[l], peer_k1[l], peer_k2[l], peer_u[l], peer_v[l])
        hp, r_mla, r_kinv, r_dsa = trunk_layer(hp, pos_p, None, rel_bias, *lw)
        mla_p.append(r_mla)
        kinv_p.append(r_kinv)
        dsa_p.append(r_dsa)
        past = (gather_pages(cache_mla, l, page_table),
                gather_pages(cache_mla_kinv, l, page_table),
                gather_pages(cache_dsa, l, page_table))
        hs, r_mla, r_kinv, r_dsa = trunk_layer(hs, pos_s, past, rel_bias, *lw)
        mla_s.append(r_mla)
        kinv_s.append(r_kinv)
        dsa_s.append(r_dsa)
    return (hp, hs, jnp.stack(mla_p), jnp.stack(kinv_p), jnp.stack(dsa_p),
            jnp.stack(mla_s), jnp.stack(kinv_s), jnp.stack(dsa_s))
```

```python
import functools
import math

import jax
import jax.numpy as jnp
from jax import lax
from jax.experimental import pallas as pl
from jax.experimental.pallas import tpu as pltpu

f32 = jnp.float32
bf16 = jnp.bfloat16
i32 = jnp.int32

HA = 16
D_NOPE = 128
D_ROPE = 64
D_QK_A = D_NOPE + D_ROPE
D_VA = 128
Q_LORA = 768
KV_LORA = 256
ROPE_THETA = 10000.0
HB = 32
HD_B = 64
H_IDX = 32
D_IDX = 64
DSA_TOPK = 256
N_BUCKETS = 32
MAX_DIST = 128
PEER_HEADS = 8
PEER_QDIM = 256
N_KEYS = 128
PEER_TOPK = 16
PAGE_SIZE = 128
EPS = 1e-6

LANES = 128
ROW_ALIGN = 16
VMEM_LIMIT = 56 * 1024 * 1024
PAGES_PER_STEP = 8
NEG_INF = float("-inf")


def _cparams(n_axes):
    return pltpu.CompilerParams(dimension_semantics=("arbitrary",) * n_axes, vmem_limit_bytes=VMEM_LIMIT)


def _row_tile(n, cap):
    best = None
    for t in range(ROW_ALIGN, min(n, cap) + 1, ROW_ALIGN):
        if n % t == 0:
            best = t
    assert best is not None, (n, cap)
    return best


def _col_tile(m, cap):
    best = None
    for t in range(LANES, min(m, cap) + 1, LANES):
        if m % t == 0:
            best = t
    assert best is not None, (m, cap)
    return best


def _dot(a, b):
    return jnp.dot(a, b, preferred_element_type=f32)


def _dot_nt(a, b):
    return lax.dot_general(a, b, (((1,), (1,)), ((), ())), preferred_element_type=f32)


def _transpose_exact(x):
    c = x.shape[1]
    eye = (lax.broadcasted_iota(i32, (c, c), 0) == lax.broadcasted_iota(i32, (c, c), 1)).astype(bf16)
    hi = x.astype(bf16)
    r1 = x - hi.astype(f32)
    mid = r1.astype(bf16)
    lo = (r1 - mid.astype(f32)).astype(bf16)
    return _dot_nt(eye, hi) + _dot_nt(eye, mid) + _dot_nt(eye, lo)


def _t5_bucket(d):
    n = jnp.maximum(d, 0)
    max_exact = N_BUCKETS // 2
    nf = jnp.maximum(n, 1).astype(f32)
    large = max_exact + (jnp.log(nf / max_exact) / math.log(MAX_DIST / max_exact)
                         * (N_BUCKETS - max_exact)).astype(i32)
    large = jnp.minimum(large, N_BUCKETS - 1)
    return jnp.where(n < max_exact, n, large)


def _float_key(x):
    u = pltpu.bitcast(x, i32)
    return u ^ ((u >> 31) & jnp.int32(0x7FFFFFFF))


def _topk_mask(sc, kpos, valid, k, idx_bits):
    key = _float_key(sc)
    kf = float(k)
    r = sc.shape[0]
    int_min = jnp.iinfo(jnp.int32).min

    def count_ge(t):
        return jnp.sum((key >= t).astype(f32), axis=1, keepdims=True)

    t0 = jnp.full((r, 1), int_min, i32)
    t0 = jnp.where(count_ge(jnp.zeros_like(t0)) >= kf, jnp.zeros_like(t0), t0)

    def tbody(it, t):
        cand = t | (jnp.int32(1) << (30 - it))
        return jnp.where(count_ge(cand) >= kf, cand, t)

    t = lax.fori_loop(0, 31, tbody, t0)
    gt = key > t
    eq = key == t
    n_gt = jnp.sum(gt.astype(f32), axis=1, keepdims=True)

    def jbody(it, j):
        bit = idx_bits - 1 - it
        cand = j | ((jnp.int32(1) << bit) - 1)
        cnt = n_gt + jnp.sum((eq & (kpos <= cand)).astype(f32), axis=1, keepdims=True)
        return jnp.where(cnt >= kf, j, j | (jnp.int32(1) << bit))

    j = lax.fori_loop(0, idx_bits, jbody, jnp.zeros((r, 1), i32))
    return valid & (gt | (eq & (kpos <= j)))


def _addnorm_kernel(*refs, has_res):
    if has_res:
        h_ref, r_ref, g_ref, x_ref, xn_ref = refs
        x = h_ref[...] + r_ref[...]
        x_ref[...] = x
    else:
        h_ref, g_ref, xn_ref = refs
        x = h_ref[...]
    y = x * lax.rsqrt(jnp.mean(x * x, axis=-1, keepdims=True) + EPS)
    xn_ref[...] = (y * g_ref[...]).astype(bf16)


def _addnorm(h, res, g, tm):
    n, d = h.shape
    row = pl.BlockSpec((tm, d), lambda i: (i, 0))
    gspec = pl.BlockSpec((1, d), lambda i: (0, 0))
    g2 = g.reshape(1, d)
    if res is None:
        xn = pl.pallas_call(
            functools.partial(_addnorm_kernel, has_res=False), grid=(n // tm,),
            in_specs=[row, gspec], out_specs=row,
            out_shape=jax.ShapeDtypeStruct((n, d), bf16), compiler_params=_cparams(1), name="norm")(h, g2)
        return None, xn
    x, xn = pl.pallas_call(
        functools.partial(_addnorm_kernel, has_res=True), grid=(n // tm,),
        in_specs=[row, row, gspec], out_specs=[row, row],
        out_shape=[jax.ShapeDtypeStruct((n, d), f32), jax.ShapeDtypeStruct((n, d), bf16)],
        compiler_params=_cparams(1), name="addnorm")(h, res, g2)
    return x, xn


def _add_kernel(a_ref, b_ref, o_ref):
    o_ref[...] = a_ref[...] + b_ref[...]


def _add(a, b, tm):
    n, d = a.shape
    row = pl.BlockSpec((tm, d), lambda i: (i, 0))
    return pl.pallas_call(_add_kernel, grid=(n // tm,), in_specs=[row, row], out_specs=row,
                          out_shape=jax.ShapeDtypeStruct((n, d), f32), compiler_params=_cparams(1), name="add")(a, b)


def _mm_kernel(*refs, has_res):
    if has_res:
        x_ref, w_ref, r_ref, o_ref = refs
        o_ref[...] = (r_ref[...] + _dot(x_ref[...], w_ref[...])).astype(o_ref.dtype)
    else:
        x_ref, w_ref, o_ref = refs
        o_ref[...] = _dot(x_ref[...], w_ref[...]).astype(o_ref.dtype)


def _mm(x, w, out_dtype, tm, tn, res=None):
    n, k = x.shape
    m = w.shape[1]
    in_specs = [pl.BlockSpec((tm, k), lambda i, j: (i, 0)), pl.BlockSpec((k, tn), lambda i, j: (0, j))]
    args = [x, w]
    if res is not None:
        in_specs.append(pl.BlockSpec((tm, tn), lambda i, j: (i, j)))
        args.append(res)
    return pl.pallas_call(
        functools.partial(_mm_kernel, has_res=res is not None), grid=(n // tm, m // tn),
        in_specs=in_specs, out_specs=pl.BlockSpec((tm, tn), lambda i, j: (i, j)),
        out_shape=jax.ShapeDtypeStruct((n, m), out_dtype), compiler_params=_cparams(2), name="matmul")(*args)


def _feat_kernel(zs_ref, zq_ref, cos_ref, sin_ref, wuq_ref, wukt_ref, wuk2_ref,
                 gcq_ref, gckv_ref, gqan_ref, gqap_ref, gkan_ref, gkap_ref, gqb_ref, gkb_ref,
                 qlat_ref, qpe_ref, mla_ref, mlab_ref, kinv_ref, qbn_ref, dsa_ref, dsab_ref, wi_ref):
    tm = zs_ref.shape[0]
    lane = lax.broadcasted_iota(i32, (tm, LANES), 1)
    lo = lane < 64
    cosq = cos_ref[...]
    sinq = sin_ref[...]

    def rot_half(x):
        return jnp.where((lane & 32) == 0, pltpu.roll(x, 96, 1), pltpu.roll(x, 32, 1))

    def rsum(x):
        return jnp.sum(x, axis=-1, keepdims=True)

    zcq = zs_ref[:, 0:Q_LORA]
    cq = zcq * lax.rsqrt(jnp.mean(zcq * zcq, axis=-1, keepdims=True) + EPS) * gcq_ref[...]
    q = _dot(cq.astype(bf16), wuq_ref[...])
    zckv = zs_ref[:, Q_LORA:Q_LORA + KV_LORA]
    ckv = zckv * lax.rsqrt(jnp.mean(zckv * zckv, axis=-1, keepdims=True) + EPS) * gckv_ref[...]
    o_kpe = Q_LORA + KV_LORA
    kpe_raw = zs_ref[:, o_kpe:o_kpe + LANES]
    knope = _dot(ckv.astype(bf16), wuk2_ref[...])
    kpe_sq = rsum(kpe_raw * kpe_raw)
    lane_h = lax.broadcasted_iota(i32, (tm, HA), 1)
    ksq = jnp.zeros((tm, HA), f32)
    for h in range(HA):
        kn = knope[:, h * D_NOPE:(h + 1) * D_NOPE]
        ksq = jnp.where(lane_h == h, rsum(kn * kn) + kpe_sq, ksq)
    kinv_ref[...] = lax.rsqrt(ksq / D_QK_A + EPS)
    kpe = kpe_raw * gkap_ref[...]
    kpe = kpe * cosq + rot_half(kpe) * sinq
    mla_ref[:, 0:KV_LORA] = ckv
    mla_ref[:, KV_LORA:KV_LORA + D_ROPE] = kpe[:, 0:D_ROPE]
    mlab_ref[:, 0:KV_LORA] = ckv.astype(bf16)
    mlab_ref[:, KV_LORA:KV_LORA + D_ROPE] = kpe[:, 0:D_ROPE].astype(bf16)

    gqan = gqan_ref[...]
    gqap = gqap_ref[...]
    gkan = gkan_ref[...]
    for p in range(HA // 2):
        xp = q[:, HA * D_NOPE + p * LANES:HA * D_NOPE + (p + 1) * LANES]
        sq = xp * xp
        s_pair = (rsum(jnp.where(lo, sq, 0.0)), rsum(jnp.where(lo, 0.0, sq)))
        invs = []
        for e in range(2):
            hh = 2 * p + e
            xn = q[:, hh * D_NOPE:(hh + 1) * D_NOPE]
            inv = lax.rsqrt((rsum(xn * xn) + s_pair[e]) / D_QK_A + EPS)
            invs.append(inv)
            qa = xn * inv * gqan
            ql = _dot((qa * gkan).astype(bf16), wukt_ref[hh])
            qlat_ref[hh] = ql.astype(bf16)
        xa = xp * jnp.where(lo, invs[0], invs[1]) * gqap
        qpe_ref[:, p * LANES:(p + 1) * LANES] = (xa * cosq + rot_half(xa) * sinq).astype(bf16)

    gqb = gqb_ref[...]
    scale_b = HD_B ** -0.5
    for p in range(HB * HD_B // LANES):
        x = zq_ref[:, p * LANES:(p + 1) * LANES]
        sq = x * x
        inv = jnp.where(lo, lax.rsqrt(rsum(jnp.where(lo, sq, 0.0)) / HD_B + EPS),
                        lax.rsqrt(rsum(jnp.where(lo, 0.0, sq)) / HD_B + EPS))
        qbn_ref[:, p * LANES:(p + 1) * LANES] = (x * inv * gqb * scale_b).astype(bf16)
    o_c = o_kpe + LANES
    t1 = zs_ref[:, o_c:o_c + LANES]
    invk = lax.rsqrt(rsum(jnp.where(lo, t1 * t1, 0.0)) / HD_B + EPS)
    t1 = jnp.where(lo, t1 * invk * gkb_ref[...], t1)
    t2 = zs_ref[:, o_c + LANES:o_c + 2 * LANES]
    dsa_ref[:, 0:LANES] = t1
    dsa_ref[:, LANES:LANES + D_IDX] = t2[:, 0:D_IDX]
    dsab_ref[:, 0:LANES] = t1.astype(bf16)
    dsab_ref[:, LANES:LANES + D_IDX] = t2[:, 0:D_IDX].astype(bf16)
    wi_ref[...] = t2[:, D_IDX:D_IDX + H_IDX] * (H_IDX * D_IDX) ** -0.5


def _features(zs, zbig, cosq, sinq, wl, tm):
    n = zs.shape[0]
    zs_w = zs.shape[1]
    qb_w = HB * HD_B
    qb_blk = (zbig.shape[1] - qb_w) // qb_w

    def row(w):
        return pl.BlockSpec((tm, w), lambda i: (i, 0))

    def full(a):
        return pl.BlockSpec(a.shape, lambda i: (0,) * a.ndim)

    gains = [wl["g_cq"], wl["g_ckv"], wl["g_qan"], wl["g_qap"], wl["g_kan"], wl["g_kap"], wl["g_qb"], wl["g_kb"]]
    in_specs = [row(zs_w), pl.BlockSpec((tm, qb_w), lambda i: (i, qb_blk)), row(LANES), row(LANES),
                full(wl["wuq"]), full(wl["wukt"]), full(wl["wuk2"])] + [full(g) for g in gains]
    out_specs = [pl.BlockSpec((HA, tm, KV_LORA), lambda i: (0, i, 0)), row(HA * D_ROPE),
                 row(KV_LORA + D_ROPE), row(KV_LORA + D_ROPE), row(HA), row(qb_w),
                 row(2 * HD_B + D_IDX), row(2 * HD_B + D_IDX), row(H_IDX)]
    out_shape = [jax.ShapeDtypeStruct((HA, n, KV_LORA), bf16), jax.ShapeDtypeStruct((n, HA * D_ROPE), bf16),
                 jax.ShapeDtypeStruct((n, KV_LORA + D_ROPE), f32), jax.ShapeDtypeStruct((n, KV_LORA + D_ROPE), bf16),
                 jax.ShapeDtypeStruct((n, HA), f32), jax.ShapeDtypeStruct((n, qb_w), bf16),
                 jax.ShapeDtypeStruct((n, 2 * HD_B + D_IDX), f32), jax.ShapeDtypeStruct((n, 2 * HD_B + D_IDX), bf16),
                 jax.ShapeDtypeStruct((n, H_IDX), f32)]
    return pl.pallas_call(_feat_kernel, grid=(n // tm,), in_specs=in_specs, out_specs=out_specs,
                          out_shape=out_shape, compiler_params=_cparams(1), name="features")(
        zs, zbig, cosq, sinq, wl["wuq"], wl["wukt"], wl["wuk2"], *gains)


def _mla_p_kernel(qlat_ref, qpe_ref, k_ref, kinvt_ref, wuv_ref, oa_ref):
    tq = qpe_ref.shape[0]
    s_len = k_ref.shape[0]
    i = pl.program_id(1)
    k = k_ref[...]
    ckv = k[:, 0:KV_LORA]
    kpe = k[:, KV_LORA:KV_LORA + D_ROPE]
    kinv = kinvt_ref[...] * (D_QK_A ** -0.5)
    qpos = i * tq + lax.broadcasted_iota(i32, (tq, s_len), 0)
    valid = lax.broadcasted_iota(i32, (tq, s_len), 1) <= qpos
    for h in range(HA):
        s = _dot_nt(qlat_ref[h], ckv) + _dot_nt(qpe_ref[:, h * D_ROPE:(h + 1) * D_ROPE], kpe)
        s = jnp.where(valid, s * kinv[h:h + 1, :], NEG_INF)
        p = jnp.exp(s - jnp.max(s, axis=-1, keepdims=True))
        p = (p * (1.0 / jnp.sum(p, axis=-1, keepdims=True))).astype(bf16)
        ctx = _dot(p, ckv)
        oa_ref[:, h * D_VA:(h + 1) * D_VA] = _dot(ctx.astype(bf16), wuv_ref[h]).astype(bf16)


def _mla_prompt(qlat, qpe, mlab, kinvt, wuv, b, t, tq):
    nq = t // tq
    return pl.pallas_call(
        _mla_p_kernel, grid=(b, nq),
        in_specs=[pl.BlockSpec((HA, tq, KV_LORA), lambda bi, i: (0, bi * nq + i, 0)),
                  pl.BlockSpec((tq, HA * D_ROPE), lambda bi, i: (bi * nq + i, 0)),
                  pl.BlockSpec((t, KV_LORA + D_ROPE), lambda bi, i: (bi, 0)),
                  pl.BlockSpec((HA, t), lambda bi, i: (0, bi)),
                  pl.BlockSpec(wuv.shape, lambda bi, i: (0, 0, 0))],
        out_specs=pl.BlockSpec((tq, HA * D_VA), lambda bi, i: (bi * nq + i, 0)),
        out_shape=jax.ShapeDtypeStruct((b * t, HA * D_VA), bf16), compiler_params=_cparams(2), name="mla_prompt")(
        qlat, qpe, mlab, kinvt, wuv)


def _mla_s_kernel(pt_ref, ql_ref, qp_ref, kown_ref, kinvown_ref, wuvf_ref, *rest, pg):
    pages = rest[:pg]
    kinvp = rest[pg:2 * pg]
    o_ref = rest[2 * pg]
    m_ref, l_ref, acc_ref = rest[2 * pg + 1:]
    c = pl.program_id(1)
    scale = D_QK_A ** -0.5
    ql = ql_ref[...]
    qp = qp_ref[...]

    @pl.when(c == 0)
    def _():
        ko = kown_ref[...].astype(f32)
        s0 = (jnp.sum(ql.astype(f32) * ko[:, 0:KV_LORA], axis=-1, keepdims=True)
              + jnp.sum(qp.astype(f32) * ko[:, KV_LORA:KV_LORA + D_ROPE], axis=-1, keepdims=True))
        s0 = s0 * (kinvown_ref[...] * scale)
        m_ref[...] = s0
        l_ref[...] = jnp.ones_like(s0)
        acc_ref[...] = jnp.broadcast_to(ko[:, 0:KV_LORA], acc_ref.shape)

    kc = jnp.concatenate([r[...] for r in pages], axis=0).astype(bf16)
    kinvt = _transpose_exact(jnp.concatenate([r[...] for r in kinvp], axis=0))
    s = (_dot_nt(ql, kc[:, 0:KV_LORA]) + _dot_nt(qp, kc[:, KV_LORA:KV_LORA + D_ROPE])) * (kinvt * scale)
    m_old = m_ref[...]
    m_new = jnp.maximum(m_old, jnp.max(s, axis=-1, keepdims=True))
    alpha = jnp.exp(m_old - m_new)
    p = jnp.exp(s - m_new)
    l_ref[...] = alpha * l_ref[...] + jnp.sum(p, axis=-1, keepdims=True)
    acc_ref[...] = alpha * acc_ref[...] + _dot(p.astype(bf16), kc[:, 0:KV_LORA])
    m_ref[...] = m_new

    @pl.when(c == pl.num_programs(1) - 1)
    def _():
        ctx = acc_ref[...] / l_ref[...]
        full = _dot(ctx.astype(bf16), wuvf_ref[...])
        rowi = lax.broadcasted_iota(i32, full.shape, 0)
        coli = lax.broadcasted_iota(i32, full.shape, 1)
        own = (coli >= rowi * D_VA) & (coli < (rowi + 1) * D_VA)
        o_ref[...] = jnp.sum(jnp.where(own, full, 0.0), axis=0, keepdims=True).astype(bf16)


def _mla_sample(page_table, cache, cache_kinv, layer, ql_s, qp_s, kown, kinvown, wuvf):
    nb, n_pages = page_table.shape
    pg = PAGES_PER_STEP
    assert n_pages % pg == 0
    row_w = cache.shape[3]

    def page_spec(width, kk):
        return pl.BlockSpec((None, None, PAGE_SIZE, width), lambda bi, c, pt: (layer, pt[bi, c * pg + kk], 0, 0))

    in_specs = [pl.BlockSpec((None, HA, KV_LORA), lambda bi, c, pt: (bi, 0, 0)),
                pl.BlockSpec((None, HA, D_ROPE), lambda bi, c, pt: (bi, 0, 0)),
                pl.BlockSpec((None, 1, row_w), lambda bi, c, pt: (bi, 0, 0)),
                pl.BlockSpec((None, HA, 1), lambda bi, c, pt: (bi, 0, 0)),
                pl.BlockSpec(wuvf.shape, lambda bi, c, pt: (0, 0))]
    in_specs += [page_spec(row_w, kk) for kk in range(pg)] + [page_spec(HA, kk) for kk in range(pg)]
    grid_spec = pltpu.PrefetchScalarGridSpec(
        num_scalar_prefetch=1, grid=(nb, n_pages // pg), in_specs=in_specs,
        out_specs=pl.BlockSpec((None, 1, HA * D_VA), lambda bi, c, pt: (bi, 0, 0)),
        scratch_shapes=[pltpu.VMEM((HA, 1), f32), pltpu.VMEM((HA, 1), f32), pltpu.VMEM((HA, KV_LORA), f32)])
    out = pl.pallas_call(
        functools.partial(_mla_s_kernel, pg=pg), grid_spec=grid_spec,
        out_shape=jax.ShapeDtypeStruct((nb, 1, HA * D_VA), bf16), compiler_params=_cparams(2), name="mla_sample")(
        page_table, ql_s, qp_s, kown, kinvown, wuvf, *([cache] * pg), *([cache_kinv] * pg))
    return out.reshape(nb, HA * D_VA)


def _dsa_p_kernel(rb_ref, qb_ref, qi_ref, wi_ref, kfar_ref, kn0_ref, kn1_ref, ob_ref, toep_ref, *, topk):
    tq = qb_ref.shape[0]
    s_len = kfar_ref.shape[0]
    sk = s_len + 2 * tq
    i = pl.program_id(1)

    @pl.when((pl.program_id(0) == 0) & (i == 0))
    def _():
        d = tq + lax.broadcasted_iota(i32, (tq, 2 * tq), 0) - lax.broadcasted_iota(i32, (tq, 2 * tq), 1)
        bucket = _t5_bucket(d)
        for h in range(HB):
            acc = jnp.zeros((tq, 2 * tq), f32)
            for bk in range(N_BUCKETS):
                acc = jnp.where(bucket == bk, rb_ref[bk, h], acc)
            toep_ref[h] = acc

    kall = jnp.concatenate([kfar_ref[...], kn0_ref[...], kn1_ref[...]], axis=0)
    kk = kall[:, 0:HD_B]
    vv = kall[:, HD_B:2 * HD_B]
    kidx = kall[:, 2 * HD_B:2 * HD_B + D_IDX]
    col = lax.broadcasted_iota(i32, (tq, sk), 1)
    qpos = i * tq + lax.broadcasted_iota(i32, (tq, sk), 0)
    far = col < s_len
    kpos = jnp.where(far, col, (i - 1) * tq + (col - s_len))
    valid = (far & (col < (i - 1) * tq)) | ((col >= s_len) & (kpos >= 0) & (kpos <= qpos))

    wi = wi_ref[...]
    sc = jnp.zeros((tq, sk), f32)
    for h in range(H_IDX):
        r = _dot_nt(qi_ref[:, h * D_IDX:(h + 1) * D_IDX], kidx)
        sc = sc + jnp.maximum(r, 0.0) * wi[:, h:h + 1]
    sc = jnp.where(valid, sc, NEG_INF)
    sel = _topk_mask(sc, kpos, valid, topk, max(1, (s_len - 1).bit_length()))
    madd = jnp.where(sel, 0.0, NEG_INF)

    for h in range(HB):
        lg = _dot_nt(qb_ref[:, h * HD_B:(h + 1) * HD_B], kk)
        lg = jnp.concatenate([lg[:, 0:s_len] + rb_ref[N_BUCKETS - 1, h], lg[:, s_len:sk] + toep_ref[h]], axis=1) + madd
        p = jnp.exp(lg - jnp.max(lg, axis=-1, keepdims=True))
        p = (p * (1.0 / jnp.sum(p, axis=-1, keepdims=True))).astype(bf16)
        ob_ref[:, h * HD_B:(h + 1) * HD_B] = _dot(p, vv).astype(bf16)


def _dsa_prompt(rel_bias, qbn, qi, wi, dsab, b, t, tq):
    nq = t // tq
    topk = max(1, min(DSA_TOPK, t // 4))
    kw = dsab.shape[1]
    return pl.pallas_call(
        functools.partial(_dsa_p_kernel, topk=topk), grid=(b, nq),
        in_specs=[pl.BlockSpec(memory_space=pltpu.SMEM),
                  pl.BlockSpec((tq, HB * HD_B), lambda bi, i: (bi * nq + i, 0)),
                  pl.BlockSpec((tq, H_IDX * D_IDX), lambda bi, i: (bi * nq + i, 0)),
                  pl.BlockSpec((tq, H_IDX), lambda bi, i: (bi * nq + i, 0)),
                  pl.BlockSpec((t, kw), lambda bi, i: (bi, 0)),
                  pl.BlockSpec((tq, kw), lambda bi, i: (bi * nq + jnp.maximum(i - 1, 0), 0)),
                  pl.BlockSpec((tq, kw), lambda bi, i: (bi * nq + i, 0))],
        out_specs=pl.BlockSpec((tq, HB * HD_B), lambda bi, i: (bi * nq + i, 0)),
        out_shape=jax.ShapeDtypeStruct((b * t, HB * HD_B), bf16),
        scratch_shapes=[pltpu.VMEM((HB, tq, 2 * tq), f32)],
        compiler_params=_cparams(2), name="dsa_prompt")(rel_bias, qbn, qi, wi, dsab, dsab, dsab)


def _dsa_s_kernel(pt_ref, rbt_ref, qb_ref, qi_ref, wi_ref, own_ref, *rest, pg, past, topk):
    pages = rest[:pg]
    o_ref = rest[pg]
    sc_ref, kv_ref, bias_ref = rest[pg + 1:]
    c = pl.program_id(1)
    sks = past + PAGE_SIZE
    chunk = pg * PAGE_SIZE
    qi = qi_ref[...]
    wi = wi_ref[...]

    @pl.when((pl.program_id(0) == 0) & (c == 0))
    def _():
        d = past - lax.broadcasted_iota(i32, (1, sks), 1)
        bucket = _t5_bucket(d)
        rbt = rbt_ref[...]
        acc = jnp.zeros((HB, sks), f32)
        for bk in range(N_BUCKETS):
            acc = jnp.where(bucket == bk, rbt[:, bk:bk + 1], acc)
        bias_ref[...] = acc

    def idx_score(kidx):
        r = _dot_nt(qi, kidx)
        return jnp.sum(jnp.maximum(r, 0.0) * wi, axis=0, keepdims=True)

    pgs = jnp.concatenate([r[...] for r in pages], axis=0).astype(bf16)
    off = pl.multiple_of(c * chunk, chunk)
    kv_ref[pl.ds(off, chunk), :] = pgs[:, 0:2 * HD_B]
    sc_ref[:, pl.ds(off, chunk)] = idx_score(pgs[:, 2 * HD_B:2 * HD_B + D_IDX])

    @pl.when(c == pl.num_programs(1) - 1)
    def _():
        own = own_ref[...]
        rown = lax.broadcasted_iota(i32, (PAGE_SIZE, 2 * HD_B), 0)
        own_kv = jnp.broadcast_to(own[:, 0:2 * HD_B].astype(f32), (PAGE_SIZE, 2 * HD_B))
        kv_ref[past:sks, :] = jnp.where(rown == 0, own_kv, 0.0).astype(bf16)
        lane = lax.broadcasted_iota(i32, (1, PAGE_SIZE), 1)
        r_own = jnp.sum(qi.astype(f32) * own[:, 2 * HD_B:2 * HD_B + D_IDX].astype(f32), axis=-1, keepdims=True)
        sc_own = jnp.sum(jnp.maximum(r_own, 0.0) * wi, axis=0, keepdims=True)
        sc_ref[:, past:sks] = jnp.where(lane == 0, sc_own, NEG_INF)
        kpos = lax.broadcasted_iota(i32, (1, sks), 1)
        valid = kpos <= past
        sc = jnp.where(valid, sc_ref[...], NEG_INF)
        sel = _topk_mask(sc, kpos, valid, topk, max(1, past.bit_length()))
        madd = jnp.where(sel, 0.0, NEG_INF)
        kv = kv_ref[...]
        lg = _dot_nt(qb_ref[...], kv[:, 0:HD_B]) + bias_ref[...] + madd
        p = jnp.exp(lg - jnp.max(lg, axis=-1, keepdims=True))
        p = (p * (1.0 / jnp.sum(p, axis=-1, keepdims=True))).astype(bf16)
        o_ref[...] = _dot(p, kv[:, HD_B:2 * HD_B]).astype(bf16)


def _dsa_sample(page_table, cache, layer, rbt, qb_s, qi_s, wi_s, own):
    nb, n_pages = page_table.shape
    pg = PAGES_PER_STEP
    past = n_pages * PAGE_SIZE
    sks = past + PAGE_SIZE
    topk = max(1, min(DSA_TOPK, (past + 1) // 4))
    row_w = cache.shape[3]
    in_specs = [pl.BlockSpec(rbt.shape, lambda bi, c, pt: (0, 0)),
                pl.BlockSpec((None, HB, HD_B), lambda bi, c, pt: (bi, 0, 0)),
                pl.BlockSpec((None, H_IDX, D_IDX), lambda bi, c, pt: (bi, 0, 0)),
                pl.BlockSpec((None, H_IDX, 1), lambda bi, c, pt: (bi, 0, 0)),
                pl.BlockSpec((None, 1, row_w), lambda bi, c, pt: (bi, 0, 0))]
    in_specs += [pl.BlockSpec((None, None, PAGE_SIZE, row_w),
                              functools.partial(lambda bi, c, pt, kk: (layer, pt[bi, c * pg + kk], 0, 0), kk=kk))
                 for kk in range(pg)]
    grid_spec = pltpu.PrefetchScalarGridSpec(
        num_scalar_prefetch=1, grid=(nb, n_pages // pg), in_specs=in_specs,
        out_specs=pl.BlockSpec((None, HB, HD_B), lambda bi, c, pt: (bi, 0, 0)),
        scratch_shapes=[pltpu.VMEM((1, sks), f32), pltpu.VMEM((sks, 2 * HD_B), bf16), pltpu.VMEM((HB, sks), f32)])
    out = pl.pallas_call(
        functools.partial(_dsa_s_kernel, pg=pg, past=past, topk=topk), grid_spec=grid_spec,
        out_shape=jax.ShapeDtypeStruct((nb, HB, HD_B), bf16), compiler_params=_cparams(2), name="dsa_sample")(
        page_table, rbt, qb_s, qi_s, wi_s, own, *([cache] * pg))
    return out.reshape(nb, HB * HD_B)


def _merge_kernel(oa_ref, ob_ref, wa_ref, wb_ref, ga_ref, gb_ref, o_ref):
    ga = 1.0 / (1.0 + jnp.exp(-ga_ref[...]))
    gb = 1.0 / (1.0 + jnp.exp(-gb_ref[...]))
    o_ref[...] = (ga * _dot(oa_ref[...], wa_ref[...]) + gb * _dot(ob_ref[...], wb_ref[...])).astype(bf16)


def _merge(oa, ob, wa, wb, zbig, d, tm, tn):
    n = oa.shape[0]
    nj = d // tn
    return pl.pallas_call(
        _merge_kernel, grid=(n // tm, nj),
        in_specs=[pl.BlockSpec((tm, oa.shape[1]), lambda i, j: (i, 0)),
                  pl.BlockSpec((tm, ob.shape[1]), lambda i, j: (i, 0)),
                  pl.BlockSpec((wa.shape[0], tn), lambda i, j: (0, j)),
                  pl.BlockSpec((wb.shape[0], tn), lambda i, j: (0, j)),
                  pl.BlockSpec((tm, tn), lambda i, j: (i, j)),
                  pl.BlockSpec((tm, tn), lambda i, j: (i, nj + j))],
        out_specs=pl.BlockSpec((tm, tn), lambda i, j: (i, j)),
        out_shape=jax.ShapeDtypeStruct((n, d), bf16), compiler_params=_cparams(2), name="merge")(
        oa, ob, wa, wb, zbig, zbig)


def _top_rows(s, k):
    r = s.shape[0]
    ridx = lax.broadcasted_iota(i32, s.shape, 0)
    vals, idxs = [], []
    for _ in range(k):
        m = jnp.max(s, axis=0, keepdims=True)
        am = jnp.min(jnp.where(s == m, ridx, r), axis=0, keepdims=True)
        vals.append(m)
        idxs.append(am)
        s = jnp.where(ridx == am, NEG_INF, s)
    return jnp.concatenate(vals, axis=0), jnp.concatenate(idxs, axis=0)


def _pick_rows(tab, sel):
    out = jnp.zeros(sel.shape, tab.dtype)
    for i in range(tab.shape[0]):
        out = jnp.where(sel == i, tab[i:i + 1, :], out)
    return out


def _peer_sel_kernel(q_ref, k1_ref, k2_ref, a_ref, b_ref, g_ref):
    half = PEER_QDIM // 2
    shift = PEER_TOPK.bit_length() - 1
    assert 1 << shift == PEER_TOPK
    k1 = k1_ref[...]
    k2 = k2_ref[...]
    a_rows, b_rows, g_rows = [], [], []
    for h in range(PEER_HEADS):
        q1 = q_ref[:, h * PEER_QDIM:h * PEER_QDIM + half]
        q2 = q_ref[:, h * PEER_QDIM + half:(h + 1) * PEER_QDIM]
        v1, i1 = _top_rows(_dot_nt(k1, q1), PEER_TOPK)
        v2, i2 = _top_rows(_dot_nt(k2, q2), PEER_TOPK)
        cand = jnp.concatenate([v1[i:i + 1, :] + v2 for i in range(PEER_TOPK)], axis=0)
        best, pos = _top_rows(cand, PEER_TOPK)
        a_rows.append(_pick_rows(i1, pos >> shift))
        b_rows.append(_pick_rows(i2, pos & (PEER_TOPK - 1)))
        e = jnp.exp(best - best[0:1, :])
        g_rows.append(e / jnp.sum(e, axis=0, keepdims=True))
    a_ref[...] = jnp.concatenate(a_rows, axis=0).astype(f32).T.astype(i32)
    b_ref[...] = jnp.concatenate(b_rows, axis=0).astype(f32).T.astype(i32)
    g_ref[...] = jnp.concatenate(g_rows, axis=0).T


def _peer_select(q, k1, k2, ts):
    n = q.shape[0]
    slots = PEER_HEADS * PEER_TOPK
    row = pl.BlockSpec((ts, slots), lambda i: (i, 0))
    return pl.pallas_call(
        _peer_sel_kernel, grid=(n // ts,),
        in_specs=[pl.BlockSpec((ts, q.shape[1]), lambda i: (i, 0)),
                  pl.BlockSpec(k1.shape, lambda i: (0, 0)), pl.BlockSpec(k2.shape, lambda i: (0, 0))],
        out_specs=[row, row, row],
        out_shape=[jax.ShapeDtypeStruct((n, slots), i32), jax.ShapeDtypeStruct((n, slots), i32),
                   jax.ShapeDtypeStruct((n, slots), f32)],
        compiler_params=_cparams(1), name="peer_select")(q, k1, k2)


def _peer_gate_kernel(a_ref, b_ref, g_ref, o_ref):
    ts, slots = a_ref.shape
    krow = lax.broadcasted_iota(i32, (N_KEYS, slots), 0)

    def body(n, carry):
        a = jnp.broadcast_to(a_ref[pl.ds(n, 1), :], (N_KEYS, slots))
        b = jnp.broadcast_to(b_ref[pl.ds(n, 1), :], (N_KEYS, slots))
        g = jnp.broadcast_to(g_ref[pl.ds(n, 1), :], (N_KEYS, slots))
        ma = jnp.where(a == krow, g, 0.0).astype(bf16)
        mb = jnp.where(b == krow, 1.0, 0.0).astype(bf16)
        o_ref[n] = _dot_nt(ma, mb).astype(bf16)
        return carry

    lax.fori_loop(0, ts, body, 0)


def _peer_gates(a, b, g, ts):
    n, slots = a.shape
    row = pl.BlockSpec((ts, slots), lambda i: (i, 0))
    return pl.pallas_call(
        _peer_gate_kernel, grid=(n // ts,), in_specs=[row, row, row],
        out_specs=pl.BlockSpec((ts, N_KEYS, N_KEYS), lambda i: (i, 0, 0)),
        out_shape=jax.ShapeDtypeStruct((n, N_KEYS, N_KEYS), bf16), compiler_params=_cparams(1), name="peer_gates")(a, b, g)


def _peer_dense_kernel(xn_ref, gd_ref, u_ref, v_ref, o_ref):
    @pl.when(pl.program_id(1) == 0)
    def _():
        o_ref[...] = jnp.zeros_like(o_ref)

    hid = _dot_nt(xn_ref[...], u_ref[...])
    act = gd_ref[...].astype(f32) * (0.5 * hid * (1.0 + lax.erf(hid * (2.0 ** -0.5))))
    o_ref[...] += _dot(act.astype(bf16), v_ref[...])


def _peer_dense(xn, gd, u, v, tm, te):
    n, d = xn.shape
    ne = u.shape[0]
    return pl.pallas_call(
        _peer_dense_kernel, grid=(n // tm, ne // te),
        in_specs=[pl.BlockSpec((tm, d), lambda i, e: (i, 0)), pl.BlockSpec((tm, te), lambda i, e: (i, e)),
                  pl.BlockSpec((te, d), lambda i, e: (e, 0)), pl.BlockSpec((te, d), lambda i, e: (e, 0))],
        out_specs=pl.BlockSpec((tm, d), lambda i, e: (i, 0)),
        out_shape=jax.ShapeDtypeStruct((n, d), f32), compiler_params=_cparams(2), name="peer_dense")(xn, gd, u, v)


def _prep_layer(l, norm1_g, w_in, cq_norm_g, ckv_norm_g, w_uq, w_uk, w_uv, q_norm_a, k_norm_a,
                q_norm_b, k_norm_b, w_up_a, w_up_b, w_out, norm2_g, peer_wq, peer_k1, peer_k2, peer_u, peer_v):
    d = w_in.shape[1]
    win = w_in[l]
    splits = (Q_LORA, KV_LORA, D_ROPE, HB * HD_B, HD_B, HD_B, H_IDX * D_IDX, D_IDX, H_IDX, d, d)
    offs = [0]
    for s in splits:
        offs.append(offs[-1] + s)
    cq, ckv, kpe, qb, kb, vb, qi, ki, wi, ga, gb = [win[:, offs[i]:offs[i + 1]] for i in range(len(splits))]

    def zeros(w):
        return jnp.zeros((d, w), win.dtype)

    small = [cq, ckv, kpe, zeros(LANES - D_ROPE), kb, vb, ki, wi, zeros(LANES - D_IDX - H_IDX)]
    zs_w = sum(a.shape[1] for a in small)
    pad = (-zs_w) % 512
    w_s = jnp.concatenate(small + ([zeros(pad)] if pad else []), axis=1).astype(bf16)
    w_big = jnp.concatenate([ga, gb, qb], axis=1).astype(bf16)
    wuq3 = w_uq[l].reshape(Q_LORA, HA, D_QK_A)
    wuq = jnp.concatenate([wuq3[:, :, :D_NOPE].reshape(Q_LORA, HA * D_NOPE),
                           wuq3[:, :, D_NOPE:].reshape(Q_LORA, HA * D_ROPE)], axis=1).astype(bf16)
    ones_half = jnp.ones((LANES - HD_B,), f32)
    return dict(
        norm1_g=norm1_g[l], norm2_g=norm2_g[l], w_s=w_s, w_big=w_big, w_qi=qi.astype(bf16),
        wuq=wuq, wukt=jnp.transpose(w_uk[l], (1, 2, 0)).astype(bf16),
        wuk2=w_uk[l].reshape(KV_LORA, HA * D_NOPE).astype(bf16),
        wuv=jnp.transpose(w_uv[l], (1, 0, 2)).astype(bf16), wuvf=w_uv[l].reshape(KV_LORA, HA * D_VA).astype(bf16),
        g_cq=cq_norm_g[l].reshape(1, Q_LORA), g_ckv=ckv_norm_g[l].reshape(1, KV_LORA),
        g_qan=q_norm_a[l][:D_NOPE].reshape(1, D_NOPE), g_qap=jnp.tile(q_norm_a[l][D_NOPE:], 2).reshape(1, LANES),
        g_kan=k_norm_a[l][:D_NOPE].reshape(1, D_NOPE),
        g_kap=jnp.concatenate([k_norm_a[l][D_NOPE:], jnp.zeros((LANES - D_ROPE,), f32)]).reshape(1, LANES),
        g_qb=jnp.tile(q_norm_b[l], LANES // HD_B).reshape(1, LANES),
        g_kb=jnp.concatenate([k_norm_b[l], ones_half]).reshape(1, LANES),
        w_up_a=w_up_a[l].astype(bf16), w_up_b=w_up_b[l].astype(bf16), w_out=w_out[l].astype(bf16),
        peer_wq=peer_wq[l].astype(bf16), peer_k1=peer_k1[l].astype(bf16), peer_k2=peer_k2[l].astype(bf16),
        peer_u=peer_u[l].astype(bf16), peer_v=peer_v[l].astype(bf16))


def _rope_tables(pos):
    half = D_ROPE // 2
    inv_freq = ROPE_THETA ** (-jnp.arange(half, dtype=f32) / half)
    ang = pos.astype(f32)[:, None] * inv_freq[None, :]
    cos, sin = jnp.cos(ang), jnp.sin(ang)
    return jnp.tile(jnp.concatenate([cos, cos], axis=1), (1, 2)), jnp.tile(jnp.concatenate([-sin, sin], axis=1), (1, 2))


def kernel(x_prompt, x_sample, cache_mla, cache_mla_kinv, cache_dsa, page_table, rel_bias, norm1_g, w_in, cq_norm_g, ckv_norm_g, w_uq, w_uk, w_uv, q_norm_a, k_norm_a, q_norm_b, k_norm_b, w_up_a, w_up_b, w_out, norm2_g, peer_wq, peer_k1, peer_k2, peer_u, peer_v):
    b, t, d = x_prompt.shape
    nb, ts_, _ = x_sample.shape
    depth = w_in.shape[0]
    assert ts_ == 1 and HA % 2 == 0 and Q_LORA % LANES == 0 and KV_LORA % LANES == 0
    assert cache_mla.shape[2] == PAGE_SIZE and N_KEYS == LANES and PEER_HEADS * PEER_TOPK == LANES
    npr = b * t
    n = npr + nb
    past = page_table.shape[1] * PAGE_SIZE

    tm_big = _row_tile(n, 832)
    tm_feat = _row_tile(n, 208)
    tm_small = _row_tile(n, 208)
    tm_ffn = _row_tile(n, 640)
    tq_a = _row_tile(t, 256)
    tq_b = LANES
    assert t % tq_b == 0 and n % LANES == 0

    pos = jnp.concatenate([jnp.tile(jnp.arange(t, dtype=i32), b), jnp.full((nb,), past, i32)])
    cosq, sinq = _rope_tables(pos)
    rbt = rel_bias.T

    h_prev = jnp.concatenate([x_prompt.reshape(npr, d), x_sample.reshape(nb, d)], axis=0)
    ffn_prev = None
    rows = {k: [] for k in ("mla", "kinv", "dsa")}
    for l in range(depth):
        wl = _prep_layer(l, norm1_g, w_in, cq_norm_g, ckv_norm_g, w_uq, w_uk, w_uv, q_norm_a, k_norm_a,
                         q_norm_b, k_norm_b, w_up_a, w_up_b, w_out, norm2_g, peer_wq, peer_k1, peer_k2, peer_u, peer_v)
        x, xn = _addnorm(h_prev, ffn_prev, wl["norm1_g"], tm_small)
        if x is None:
            x = h_prev
        zs = _mm(xn, wl["w_s"], f32, tm_big, _col_tile(wl["w_s"].shape[1], 512))
        zbig = _mm(xn, wl["w_big"], f32, tm_big, _col_tile(math.gcd(d, wl["w_big"].shape[1]), 512))
        qi = _mm(xn, wl["w_qi"], bf16, tm_big, _col_tile(wl["w_qi"].shape[1], 512))
        qlat, qpe, mla_row, mlab, kinv, qbn, dsa_row, dsab, wi = _features(zs, zbig, cosq, sinq, wl, tm_feat)
        rows["mla"].append(mla_row)
        rows["kinv"].append(kinv)
        rows["dsa"].append(dsa_row)

        oa_p = _mla_prompt(qlat, qpe, mlab, kinv.T, wl["wuv"], b, t, tq_a)
        oa_s = _mla_sample(page_table, cache_mla, cache_mla_kinv, l,
                           jnp.transpose(qlat[:, npr:, :], (1, 0, 2)), qpe[npr:].reshape(nb, HA, D_ROPE),
                           mlab[npr:].reshape(nb, 1, KV_LORA + D_ROPE), kinv[npr:].reshape(nb, HA, 1), wl["wuvf"])
        ob_p = _dsa_prompt(rel_bias, qbn, qi, wi, dsab, b, t, tq_b)
        ob_s = _dsa_sample(page_table, cache_dsa, l, rbt, qbn[npr:].reshape(nb, HB, HD_B),
                           qi[npr:].reshape(nb, H_IDX, D_IDX), wi[npr:].reshape(nb, H_IDX, 1),
                           dsab[npr:].reshape(nb, 1, 2 * HD_B + D_IDX))
        oa = jnp.concatenate([oa_p, oa_s], axis=0)
        ob = jnp.concatenate([ob_p, ob_s], axis=0)
        tn_d = _col_tile(d, 512)
        mix = _merge(oa, ob, wl["w_up_a"], wl["w_up_b"], zbig, d, tm_big, tn_d)
        h = _mm(mix, wl["w_out"], f32, tm_big, tn_d, res=x)

        _, xn2 = _addnorm(h, None, wl["norm2_g"], tm_small)
        pq = _mm(xn2, wl["peer_wq"], bf16, tm_big, _col_tile(wl["peer_wq"].shape[1], 512))
        ea, eb, eg = _peer_select(pq, wl["peer_k1"], wl["peer_k2"], LANES)
        gd = _peer_gates(ea, eb, eg, LANES).reshape(n, N_KEYS * N_KEYS)
        ffn_prev = _peer_dense(xn2, gd, wl["peer_u"], wl["peer_v"], tm_ffn, 512)
        h_prev = h

    y = _add(h_prev, ffn_prev, tm_small)

    def stack(key, lo, hi, shape):
        return jnp.stack([r[lo:hi].reshape(shape + (r.shape[1],)) for r in rows[key]])

    return (y[:npr].reshape(b, t, d), y[npr:].reshape(nb, 1, d),
            stack("mla", 0, npr, (b, t)), stack("kinv", 0, npr, (b, t)), stack("dsa", 0, npr, (b, t)),
            stack("mla", npr, n, (nb, 1)), stack("kinv", npr, n, (nb, 1)), stack("dsa", npr, n, (nb, 1)))
```

```python
import functools
import math

import jax
import jax.numpy as jnp
from jax import lax
from jax.experimental import pallas as pl
from jax.experimental.pallas import tpu as pltpu

f32 = jnp.float32
bf16 = jnp.bfloat16
i32 = jnp.int32

HA = 16
D_NOPE = 128
D_ROPE = 64
D_QK_A = D_NOPE + D_ROPE
D_VA = 128
Q_LORA = 768
KV_LORA = 256
ROPE_THETA = 10000.0
HB = 32
HD_B = 64
H_IDX = 32
D_IDX = 64
DSA_TOPK = 256
N_BUCKETS = 32
MAX_DIST = 128
PEER_HEADS = 8
PEER_QDIM = 256
N_KEYS = 128
PEER_TOPK = 16
PAGE_SIZE = 128
EPS = 1e-6

LANES = 128
ROW_ALIGN = 16
VMEM_LIMIT = 56 * 1024 * 1024
PAGES_PER_STEP = 16
DSA_GROUP = 2
NEG_INF = float("-inf")


def _cparams(n_axes):
    return pltpu.CompilerParams(dimension_semantics=("arbitrary",) * n_axes, vmem_limit_bytes=VMEM_LIMIT)


def _row_tile(n, cap):
    best = None
    for t in range(ROW_ALIGN, min(n, cap) + 1, ROW_ALIGN):
        if n % t == 0:
            best = t
    assert best is not None, (n, cap)
    return best


def _col_tile(m, cap):
    best = None
    for t in range(LANES, min(m, cap) + 1, LANES):
        if m % t == 0:
            best = t
    assert best is not None, (m, cap)
    return best


def _dot(a, b):
    return jnp.dot(a, b, preferred_element_type=f32)


def _dot_nt(a, b):
    return lax.dot_general(a, b, (((1,), (1,)), ((), ())), preferred_element_type=f32)


def _transpose_exact(x):
    c = x.shape[1]
    eye = (lax.broadcasted_iota(i32, (c, c), 0) == lax.broadcasted_iota(i32, (c, c), 1)).astype(bf16)
    hi = x.astype(bf16)
    r1 = x - hi.astype(f32)
    mid = r1.astype(bf16)
    lo = (r1 - mid.astype(f32)).astype(bf16)
    return _dot_nt(eye, hi) + _dot_nt(eye, mid) + _dot_nt(eye, lo)


def _t5_bucket(d):
    n = jnp.maximum(d, 0)
    max_exact = N_BUCKETS // 2
    nf = jnp.maximum(n, 1).astype(f32)
    large = max_exact + (jnp.log(nf / max_exact) / math.log(MAX_DIST / max_exact)
                         * (N_BUCKETS - max_exact)).astype(i32)
    large = jnp.minimum(large, N_BUCKETS - 1)
    return jnp.where(n < max_exact, n, large)


def _float_key(x):
    u = pltpu.bitcast(x, i32)
    return u ^ ((u >> 31) & jnp.int32(0x7FFFFFFF))


def _topk_mask(sc, kpos, valid, k, idx_bits):
    key = _float_key(sc)
    kf = float(k)
    r = sc.shape[0]
    int_min = jnp.iinfo(jnp.int32).min

    def count_ge(t):
        return jnp.sum((key >= t).astype(f32), axis=1, keepdims=True)

    t0 = jnp.full((r, 1), int_min, i32)
    t0 = jnp.where(count_ge(jnp.zeros_like(t0)) >= kf, jnp.zeros_like(t0), t0)

    def tbody(it, t):
        cand = t | (jnp.int32(1) << (30 - it))
        return jnp.where(count_ge(cand) >= kf, cand, t)

    t = lax.fori_loop(0, 31, tbody, t0)
    gt = key > t
    eq = key == t
    n_gt = jnp.sum(gt.astype(f32), axis=1, keepdims=True)

    def jbody(it, j):
        bit = idx_bits - 1 - it
        cand = j | ((jnp.int32(1) << bit) - 1)
        cnt = n_gt + jnp.sum((eq & (kpos <= cand)).astype(f32), axis=1, keepdims=True)
        return jnp.where(cnt >= kf, j, j | (jnp.int32(1) << bit))

    j = lax.fori_loop(0, idx_bits, jbody, jnp.zeros((r, 1), i32))
    return valid & (gt | (eq & (kpos <= j)))


def _topk_mask_row(sc, kpos, valid, k, idx_bits):
    radix_bits = 3
    nc = 1 << radix_bits
    c = sc.shape[1]
    kf = float(k)
    key = jnp.broadcast_to(_float_key(sc), (nc, c))
    j = lax.broadcasted_iota(i32, (nc, 1), 0)

    def count(mask):
        return jnp.sum(mask.astype(f32), axis=1, keepdims=True)

    def best(ok, jj):
        return jnp.max(jnp.where(ok, jj, 0), axis=0, keepdims=True)

    nonneg = count(key >= 0)[0:1, :] >= kf
    t = jnp.where(nonneg, 0, jnp.iinfo(jnp.int32).min).astype(i32)
    for shift in range(31 - radix_bits, -1, -radix_bits):
        t = t + (best(count(key >= t + (j << shift)) >= kf, j) << shift)
    j1 = jnp.minimum(j, 1)
    t = t + best(count(key >= t + j1) >= kf, j1)

    gt = key > t
    eq = key == t
    n_gt = count(gt)
    kposb = jnp.broadcast_to(kpos, (nc, c))

    def below(cands):
        return n_gt + count(eq & (kposb <= cands)) < kf

    m = jnp.zeros((1, 1), i32)
    nsteps = -(-idx_bits // radix_bits)
    for s in range(nsteps):
        shift = radix_bits * (nsteps - 1 - s)
        m = m + (best(below(m + (j << shift)), j) << shift)
    jstar = jnp.where(below(jnp.broadcast_to(m, (nc, 1)))[0:1, :], m + 1, m)
    return valid & (gt[0:1, :] | (eq[0:1, :] & (kpos <= jstar)))


def _addnorm_kernel(*refs, has_res):
    if has_res:
        h_ref, r_ref, g_ref, x_ref, xn_ref = refs
        x = h_ref[...] + r_ref[...]
        x_ref[...] = x
    else:
        h_ref, g_ref, xn_ref = refs
        x = h_ref[...]
    y = x * lax.rsqrt(jnp.mean(x * x, axis=-1, keepdims=True) + EPS)
    xn_ref[...] = (y * g_ref[...]).astype(bf16)


def _addnorm(h, res, g, tm):
    n, d = h.shape
    row = pl.BlockSpec((tm, d), lambda i: (i, 0))
    gspec = pl.BlockSpec((1, d), lambda i: (0, 0))
    g2 = g.reshape(1, d)
    if res is None:
        xn = pl.pallas_call(
            functools.partial(_addnorm_kernel, has_res=False), grid=(n // tm,),
            in_specs=[row, gspec], out_specs=row,
            out_shape=jax.ShapeDtypeStruct((n, d), bf16), compiler_params=_cparams(1), name="norm")(h, g2)
        return None, xn
    x, xn = pl.pallas_call(
        functools.partial(_addnorm_kernel, has_res=True), grid=(n // tm,),
        in_specs=[row, row, gspec], out_specs=[row, row],
        out_shape=[jax.ShapeDtypeStruct((n, d), f32), jax.ShapeDtypeStruct((n, d), bf16)],
        compiler_params=_cparams(1), name="addnorm")(h, res, g2)
    return x, xn


def _add_kernel(a_ref, b_ref, o_ref):
    o_ref[...] = a_ref[...] + b_ref[...]


def _add(a, b, tm):
    n, d = a.shape
    row = pl.BlockSpec((tm, d), lambda i: (i, 0))
    return pl.pallas_call(_add_kernel, grid=(n // tm,), in_specs=[row, row], out_specs=row,
                          out_shape=jax.ShapeDtypeStruct((n, d), f32), compiler_params=_cparams(1), name="add")(a, b)


def _mm_kernel(*refs, has_res):
    if has_res:
        x_ref, w_ref, r_ref, o_ref = refs
        o_ref[...] = (r_ref[...] + _dot(x_ref[...], w_ref[...])).astype(o_ref.dtype)
    else:
        x_ref, w_ref, o_ref = refs
        o_ref[...] = _dot(x_ref[...], w_ref[...]).astype(o_ref.dtype)


def _mm(x, w, out_dtype, tm, tn, res=None):
    n, k = x.shape
    m = w.shape[1]
    in_specs = [pl.BlockSpec((tm, k), lambda i, j: (i, 0)), pl.BlockSpec((k, tn), lambda i, j: (0, j))]
    args = [x, w]
    if res is not None:
        in_specs.append(pl.BlockSpec((tm, tn), lambda i, j: (i, j)))
        args.append(res)
    return pl.pallas_call(
        functools.partial(_mm_kernel, has_res=res is not None), grid=(n // tm, m // tn),
        in_specs=in_specs, out_specs=pl.BlockSpec((tm, tn), lambda i, j: (i, j)),
        out_shape=jax.ShapeDtypeStruct((n, m), out_dtype), compiler_params=_cparams(2), name="matmul")(*args)


def _feat_kernel(zs_ref, zq_ref, cos_ref, sin_ref, wuq_ref, wukt_ref, wuk2_ref,
                 gcq_ref, gckv_ref, gqan_ref, gqap_ref, gkan_ref, gkap_ref, gqb_ref, gkb_ref,
                 qlat_ref, qpe_ref, mla_ref, mlab_ref, kinv_ref, qbn_ref, dsa_ref, dsab_ref, wi_ref):
    tm = zs_ref.shape[0]
    lane = lax.broadcasted_iota(i32, (tm, LANES), 1)
    lo = lane < 64
    cosq = cos_ref[...]
    sinq = sin_ref[...]

    def rot_half(x):
        return jnp.where((lane & 32) == 0, pltpu.roll(x, 96, 1), pltpu.roll(x, 32, 1))

    def rsum(x):
        return jnp.sum(x, axis=-1, keepdims=True)

    zcq = zs_ref[:, 0:Q_LORA]
    cq = zcq * lax.rsqrt(jnp.mean(zcq * zcq, axis=-1, keepdims=True) + EPS) * gcq_ref[...]
    q = _dot(cq.astype(bf16), wuq_ref[...])
    zckv = zs_ref[:, Q_LORA:Q_LORA + KV_LORA]
    ckv = zckv * lax.rsqrt(jnp.mean(zckv * zckv, axis=-1, keepdims=True) + EPS) * gckv_ref[...]
    o_kpe = Q_LORA + KV_LORA
    kpe_raw = zs_ref[:, o_kpe:o_kpe + LANES]
    knope = _dot(ckv.astype(bf16), wuk2_ref[...])
    kpe_sq = rsum(kpe_raw * kpe_raw)
    lane_h = lax.broadcasted_iota(i32, (tm, HA), 1)
    ksq = jnp.zeros((tm, HA), f32)
    for h in range(HA):
        kn = knope[:, h * D_NOPE:(h + 1) * D_NOPE]
        ksq = jnp.where(lane_h == h, rsum(kn * kn) + kpe_sq, ksq)
    kinv_ref[...] = lax.rsqrt(ksq / D_QK_A + EPS)
    kpe = kpe_raw * gkap_ref[...]
    kpe = kpe * cosq + rot_half(kpe) * sinq
    mla_ref[:, 0:KV_LORA] = ckv
    mla_ref[:, KV_LORA:KV_LORA + D_ROPE] = kpe[:, 0:D_ROPE]
    mlab_ref[:, 0:KV_LORA] = ckv.astype(bf16)
    mlab_ref[:, KV_LORA:KV_LORA + D_ROPE] = kpe[:, 0:D_ROPE].astype(bf16)

    gqan = gqan_ref[...]
    gqap = gqap_ref[...]
    gkan = gkan_ref[...]
    for p in range(HA // 2):
        xp = q[:, HA * D_NOPE + p * LANES:HA * D_NOPE + (p + 1) * LANES]
        sq = xp * xp
        s_pair = (rsum(jnp.where(lo, sq, 0.0)), rsum(jnp.where(lo, 0.0, sq)))
        invs = []
        for e in range(2):
            hh = 2 * p + e
            xn = q[:, hh * D_NOPE:(hh + 1) * D_NOPE]
            inv = lax.rsqrt((rsum(xn * xn) + s_pair[e]) / D_QK_A + EPS)
            invs.append(inv)
            qa = xn * inv * gqan
            ql = _dot((qa * gkan).astype(bf16), wukt_ref[hh])
            qlat_ref[hh] = ql.astype(bf16)
        xa = xp * jnp.where(lo, invs[0], invs[1]) * gqap
        qpe_ref[:, p * LANES:(p + 1) * LANES] = (xa * cosq + rot_half(xa) * sinq).astype(bf16)

    gqb = gqb_ref[...]
    scale_b = HD_B ** -0.5
    for p in range(HB * HD_B // LANES):
        x = zq_ref[:, p * LANES:(p + 1) * LANES]
        sq = x * x
        inv = jnp.where(lo, lax.rsqrt(rsum(jnp.where(lo, sq, 0.0)) / HD_B + EPS),
                        lax.rsqrt(rsum(jnp.where(lo, 0.0, sq)) / HD_B + EPS))
        qbn_ref[:, p * LANES:(p + 1) * LANES] = (x * inv * gqb * scale_b).astype(bf16)
    o_c = o_kpe + LANES
    t1 = zs_ref[:, o_c:o_c + LANES]
    invk = lax.rsqrt(rsum(jnp.where(lo, t1 * t1, 0.0)) / HD_B + EPS)
    t1 = jnp.where(lo, t1 * invk * gkb_ref[...], t1)
    t2 = zs_ref[:, o_c + LANES:o_c + 2 * LANES]
    dsa_ref[:, 0:LANES] = t1
    dsa_ref[:, LANES:LANES + D_IDX] = t2[:, 0:D_IDX]
    dsab_ref[:, 0:LANES] = t1.astype(bf16)
    dsab_ref[:, LANES:LANES + D_IDX] = t2[:, 0:D_IDX].astype(bf16)
    wi_ref[...] = t2[:, D_IDX:D_IDX + H_IDX] * (H_IDX * D_IDX) ** -0.5


def _features(zs, zbig, cosq, sinq, wl, tm):
    n = zs.shape[0]
    zs_w = zs.shape[1]
    qb_w = HB * HD_B
    qb_blk = (zbig.shape[1] - qb_w) // qb_w

    def row(w):
        return pl.BlockSpec((tm, w), lambda i: (i, 0))

    def full(a):
        return pl.BlockSpec(a.shape, lambda i: (0,) * a.ndim)

    gains = [wl["g_cq"], wl["g_ckv"], wl["g_qan"], wl["g_qap"], wl["g_kan"], wl["g_kap"], wl["g_qb"], wl["g_kb"]]
    in_specs = [row(zs_w), pl.BlockSpec((tm, qb_w), lambda i: (i, qb_blk)), row(LANES), row(LANES),
                full(wl["wuq"]), full(wl["wukt"]), full(wl["wuk2"])] + [full(g) for g in gains]
    out_specs = [pl.BlockSpec((HA, tm, KV_LORA), lambda i: (0, i, 0)), row(HA * D_ROPE),
                 row(KV_LORA + D_ROPE), row(KV_LORA + D_ROPE), row(HA), row(qb_w),
                 row(2 * HD_B + D_IDX), row(2 * HD_B + D_IDX), row(H_IDX)]
    out_shape = [jax.ShapeDtypeStruct((HA, n, KV_LORA), bf16), jax.ShapeDtypeStruct((n, HA * D_ROPE), bf16),
                 jax.ShapeDtypeStruct((n, KV_LORA + D_ROPE), f32), jax.ShapeDtypeStruct((n, KV_LORA + D_ROPE), bf16),
                 jax.ShapeDtypeStruct((n, HA), f32), jax.ShapeDtypeStruct((n, qb_w), bf16),
                 jax.ShapeDtypeStruct((n, 2 * HD_B + D_IDX), f32), jax.ShapeDtypeStruct((n, 2 * HD_B + D_IDX), bf16),
                 jax.ShapeDtypeStruct((n, H_IDX), f32)]
    return pl.pallas_call(_feat_kernel, grid=(n // tm,), in_specs=in_specs, out_specs=out_specs,
                          out_shape=out_shape, compiler_params=_cparams(1), name="features")(
        zs, zbig, cosq, sinq, wl["wuq"], wl["wukt"], wl["wuk2"], *gains)


def _mla_p_kernel(qlat_ref, qpe_ref, k_ref, kinvt_ref, wuv_ref, oa_ref, *, i):
    tq = qpe_ref.shape[0]
    s_len = (i + 1) * tq
    k = k_ref[0:s_len, :]
    ckv = k[:, 0:KV_LORA]
    kpe = k[:, KV_LORA:KV_LORA + D_ROPE]
    kinv = kinvt_ref[:, 0:s_len] * (D_QK_A ** -0.5)
    qpos = i * tq + lax.broadcasted_iota(i32, (tq, s_len), 0)
    valid = lax.broadcasted_iota(i32, (tq, s_len), 1) <= qpos
    for h in range(HA):
        s = _dot_nt(qlat_ref[h], ckv) + _dot_nt(qpe_ref[:, h * D_ROPE:(h + 1) * D_ROPE], kpe)
        s = jnp.where(valid, s * kinv[h:h + 1, :], NEG_INF)
        p = jnp.exp(s - jnp.max(s, axis=-1, keepdims=True))
        p = (p * (1.0 / jnp.sum(p, axis=-1, keepdims=True))).astype(bf16)
        ctx = _dot(p, ckv)
        oa_ref[:, h * D_VA:(h + 1) * D_VA] = _dot(ctx.astype(bf16), wuv_ref[h]).astype(bf16)


def _mla_prompt(qlat, qpe, mlab, kinvt, wuv, b, t, tq):
    nq = t // tq
    outs = []
    for i in range(nq):
        outs.append(pl.pallas_call(
            functools.partial(_mla_p_kernel, i=i), grid=(b,),
            in_specs=[pl.BlockSpec((HA, tq, KV_LORA), lambda bi, i=i: (0, bi * nq + i, 0)),
                      pl.BlockSpec((tq, HA * D_ROPE), lambda bi, i=i: (bi * nq + i, 0)),
                      pl.BlockSpec((t, KV_LORA + D_ROPE), lambda bi: (bi, 0)),
                      pl.BlockSpec((HA, t), lambda bi: (0, bi)),
                      pl.BlockSpec(wuv.shape, lambda bi: (0, 0, 0))],
            out_specs=pl.BlockSpec((None, tq, HA * D_VA), lambda bi: (bi, 0, 0)),
            out_shape=jax.ShapeDtypeStruct((b, tq, HA * D_VA), bf16), compiler_params=_cparams(1),
            name="mla_prompt")(qlat, qpe, mlab, kinvt, wuv))
    return jnp.concatenate(outs, axis=1).reshape(b * t, HA * D_VA)


def _mla_s_kernel(pt_ref, ql_ref, qp_ref, kown_ref, kinvown_ref, wuvf_ref, *rest, pg):
    pages = rest[:pg]
    kinvp = rest[pg:2 * pg]
    o_ref = rest[2 * pg]
    m_ref, l_ref, acc_ref = rest[2 * pg + 1:]
    c = pl.program_id(1)
    scale = D_QK_A ** -0.5
    ql = ql_ref[...]
    qp = qp_ref[...]

    @pl.when(c == 0)
    def _():
        ko = kown_ref[...].astype(f32)
        s0 = (jnp.sum(ql.astype(f32) * ko[:, 0:KV_LORA], axis=-1, keepdims=True)
              + jnp.sum(qp.astype(f32) * ko[:, KV_LORA:KV_LORA + D_ROPE], axis=-1, keepdims=True))
        s0 = s0 * (kinvown_ref[...] * scale)
        m_ref[...] = s0
        l_ref[...] = jnp.ones_like(s0)
        acc_ref[...] = jnp.broadcast_to(ko[:, 0:KV_LORA], acc_ref.shape)

    kt = jnp.concatenate([r[...] for r in pages], axis=1).astype(bf16)
    kinvt = jnp.concatenate([r[...] for r in kinvp], axis=1)
    s = (_dot(ql, kt[0:KV_LORA, :]) + _dot(qp, kt[KV_LORA:KV_LORA + D_ROPE, :])) * (kinvt * scale)
    m_old = m_ref[...]
    m_new = jnp.maximum(m_old, jnp.max(s, axis=-1, keepdims=True))
    alpha = jnp.exp(m_old - m_new)
    p = jnp.exp(s - m_new)
    l_ref[...] = alpha * l_ref[...] + jnp.sum(p, axis=-1, keepdims=True)
    acc_ref[...] = alpha * acc_ref[...] + _dot_nt(p.astype(bf16), kt[0:KV_LORA, :])
    m_ref[...] = m_new

    @pl.when(c == pl.num_programs(1) - 1)
    def _():
        ctx = acc_ref[...] / l_ref[...]
        full = _dot(ctx.astype(bf16), wuvf_ref[...])
        rowi = lax.broadcasted_iota(i32, full.shape, 0)
        coli = lax.broadcasted_iota(i32, full.shape, 1)
        own = (coli >= rowi * D_VA) & (coli < (rowi + 1) * D_VA)
        o_ref[...] = jnp.sum(jnp.where(own, full, 0.0), axis=0, keepdims=True).astype(bf16)


def _mla_sample(page_table, cache, cache_kinv, layer, ql_s, qp_s, kown, kinvown, wuvf):
    nb, n_pages = page_table.shape
    pg = PAGES_PER_STEP
    assert n_pages % pg == 0
    row_w = cache.shape[2]

    def page_spec(width, kk):
        return pl.BlockSpec((None, None, width, PAGE_SIZE), lambda bi, c, pt: (layer, pt[bi, c * pg + kk], 0, 0))

    in_specs = [pl.BlockSpec((None, HA, KV_LORA), lambda bi, c, pt: (bi, 0, 0)),
                pl.BlockSpec((None, HA, D_ROPE), lambda bi, c, pt: (bi, 0, 0)),
                pl.BlockSpec((None, 1, row_w), lambda bi, c, pt: (bi, 0, 0)),
                pl.BlockSpec((None, HA, 1), lambda bi, c, pt: (bi, 0, 0)),
                pl.BlockSpec(wuvf.shape, lambda bi, c, pt: (0, 0))]
    in_specs += [page_spec(row_w, kk) for kk in range(pg)] + [page_spec(HA, kk) for kk in range(pg)]
    grid_spec = pltpu.PrefetchScalarGridSpec(
        num_scalar_prefetch=1, grid=(nb, n_pages // pg), in_specs=in_specs,
        out_specs=pl.BlockSpec((None, 1, HA * D_VA), lambda bi, c, pt: (bi, 0, 0)),
        scratch_shapes=[pltpu.VMEM((HA, 1), f32), pltpu.VMEM((HA, 1), f32), pltpu.VMEM((HA, KV_LORA), f32)])
    out = pl.pallas_call(
        functools.partial(_mla_s_kernel, pg=pg), grid_spec=grid_spec,
        out_shape=jax.ShapeDtypeStruct((nb, 1, HA * D_VA), bf16), compiler_params=_cparams(2), name="mla_sample")(
        page_table, ql_s, qp_s, kown, kinvown, wuvf, *([cache] * pg), *([cache_kinv] * pg))
    return out.reshape(nb, HA * D_VA)


def _dsa_p_kernel(rb_ref, qb_ref, qi_ref, wi_ref, kfar_ref, kn0_ref, kn1_ref, ob_ref, toep_ref, *, topk, i0, far_len):
    tq = qb_ref.shape[0]
    t_len = kfar_ref.shape[0]
    s_len = far_len
    sk = s_len + 2 * tq
    i = i0 + pl.program_id(1)

    @pl.when((pl.program_id(0) == 0) & (pl.program_id(1) == 0))
    def _():
        d = tq + lax.broadcasted_iota(i32, (tq, 2 * tq), 0) - lax.broadcasted_iota(i32, (tq, 2 * tq), 1)
        bucket = _t5_bucket(d)
        for h in range(HB):
            acc = jnp.zeros((tq, 2 * tq), f32)
            for bk in range(N_BUCKETS):
                acc = jnp.where(bucket == bk, rb_ref[bk, h], acc)
            toep_ref[h] = acc

    parts = ([kfar_ref[0:far_len, :]] if far_len else []) + [kn0_ref[...], kn1_ref[...]]
    kall = jnp.concatenate(parts, axis=0)
    kk = kall[:, 0:HD_B]
    vv = kall[:, HD_B:2 * HD_B]
    kidx = kall[:, 2 * HD_B:2 * HD_B + D_IDX]
    col = lax.broadcasted_iota(i32, (tq, sk), 1)
    qpos = i * tq + lax.broadcasted_iota(i32, (tq, sk), 0)
    far = col < s_len
    kpos = jnp.where(far, col, (i - 1) * tq + (col - s_len))
    valid = (far & (col < (i - 1) * tq)) | ((col >= s_len) & (kpos >= 0) & (kpos <= qpos))

    wi = wi_ref[...]
    sc = jnp.zeros((tq, sk), f32)
    for h in range(H_IDX):
        r = _dot_nt(qi_ref[:, h * D_IDX:(h + 1) * D_IDX], kidx)
        sc = sc + jnp.maximum(r, 0.0) * wi[:, h:h + 1]
    sc = jnp.where(valid, sc, NEG_INF)
    sel = _topk_mask(sc, kpos, valid, topk, max(1, (t_len - 1).bit_length()))
    madd = jnp.where(sel, 0.0, NEG_INF)

    for h in range(HB):
        lg = _dot_nt(qb_ref[:, h * HD_B:(h + 1) * HD_B], kk)
        if far_len:
            lg = jnp.concatenate([lg[:, 0:s_len] + rb_ref[N_BUCKETS - 1, h], lg[:, s_len:sk] + toep_ref[h]], axis=1)
        else:
            lg = lg + toep_ref[h]
        lg = lg + madd
        p = jnp.exp(lg - jnp.max(lg, axis=-1, keepdims=True))
        p = (p * (1.0 / jnp.sum(p, axis=-1, keepdims=True))).astype(bf16)
        ob_ref[:, h * HD_B:(h + 1) * HD_B] = _dot(p, vv).astype(bf16)


def _dsa_prompt(rel_bias, qbn, qi, wi, dsab, b, t, tq):
    nq = t // tq
    topk = max(1, min(DSA_TOPK, t // 4))
    kw = dsab.shape[1]
    gs = DSA_GROUP if nq % DSA_GROUP == 0 else 1
    outs = []
    for i0 in range(0, nq, gs):
        far_len = max(0, i0 + gs - 2) * tq
        outs.append(pl.pallas_call(
            functools.partial(_dsa_p_kernel, topk=topk, i0=i0, far_len=far_len), grid=(b, gs),
            in_specs=[pl.BlockSpec(memory_space=pltpu.SMEM),
                      pl.BlockSpec((tq, HB * HD_B), lambda bi, j, i0=i0: (bi * nq + i0 + j, 0)),
                      pl.BlockSpec((tq, H_IDX * D_IDX), lambda bi, j, i0=i0: (bi * nq + i0 + j, 0)),
                      pl.BlockSpec((tq, H_IDX), lambda bi, j, i0=i0: (bi * nq + i0 + j, 0)),
                      pl.BlockSpec((t, kw), lambda bi, j: (bi, 0)),
                      pl.BlockSpec((tq, kw), lambda bi, j, i0=i0: (bi * nq + jnp.maximum(i0 + j - 1, 0), 0)),
                      pl.BlockSpec((tq, kw), lambda bi, j, i0=i0: (bi * nq + i0 + j, 0))],
            out_specs=pl.BlockSpec((None, tq, HB * HD_B), lambda bi, j: (bi, j, 0)),
            out_shape=jax.ShapeDtypeStruct((b, gs * tq, HB * HD_B), bf16),
            scratch_shapes=[pltpu.VMEM((HB, tq, 2 * tq), f32)],
            compiler_params=_cparams(2), name="dsa_prompt")(rel_bias, qbn, qi, wi, dsab, dsab, dsab))
    return jnp.concatenate(outs, axis=1).reshape(b * t, HB * HD_B)


def _dsa_s_kernel(pt_ref, rbt_ref, qb_ref, qi_ref, wi_ref, own_ref, ownt_ref, *rest, pg, past, topk):
    pages = rest[:pg]
    o_ref = rest[pg]
    sc_ref, kvt_ref, bias_ref = rest[pg + 1:]
    c = pl.program_id(1)
    sks = past + PAGE_SIZE
    chunk = pg * PAGE_SIZE
    qi = qi_ref[...]
    wi = wi_ref[...]

    @pl.when((pl.program_id(0) == 0) & (c == 0))
    def _():
        d = past - lax.broadcasted_iota(i32, (1, sks), 1)
        bucket = _t5_bucket(d)
        rbt = rbt_ref[...]
        acc = jnp.zeros((HB, sks), f32)
        for bk in range(N_BUCKETS):
            acc = jnp.where(bucket == bk, rbt[:, bk:bk + 1], acc)
        bias_ref[...] = acc

    pgs = jnp.concatenate([r[...] for r in pages], axis=1).astype(bf16)
    off = pl.multiple_of(c * chunk, chunk)
    kvt_ref[:, pl.ds(off, chunk)] = pgs[0:2 * HD_B, :]
    r_idx = _dot(qi, pgs[2 * HD_B:2 * HD_B + D_IDX, :])
    sc_ref[:, pl.ds(off, chunk)] = jnp.sum(jnp.maximum(r_idx, 0.0) * wi, axis=0, keepdims=True)

    @pl.when(c == pl.num_programs(1) - 1)
    def _():
        lane_kv = lax.broadcasted_iota(i32, (2 * HD_B, PAGE_SIZE), 1)
        own_col = jnp.broadcast_to(ownt_ref[0:2 * HD_B, :].astype(f32), (2 * HD_B, PAGE_SIZE))
        kvt_ref[:, past:sks] = jnp.where(lane_kv == 0, own_col, 0.0).astype(bf16)
        own = own_ref[...]
        lane = lax.broadcasted_iota(i32, (1, PAGE_SIZE), 1)
        r_own = jnp.sum(qi.astype(f32) * own[:, 2 * HD_B:2 * HD_B + D_IDX].astype(f32), axis=-1, keepdims=True)
        sc_own = jnp.sum(jnp.maximum(r_own, 0.0) * wi, axis=0, keepdims=True)
        sc_ref[:, past:sks] = jnp.where(lane == 0, sc_own, NEG_INF)
        kpos = lax.broadcasted_iota(i32, (1, sks), 1)
        valid = kpos <= past
        sc = jnp.where(valid, sc_ref[...], NEG_INF)
        sel = _topk_mask_row(sc, kpos, valid, topk, max(1, past.bit_length()))
        madd = jnp.where(sel, 0.0, NEG_INF)
        kvt = kvt_ref[...]
        lg = _dot(qb_ref[...], kvt[0:HD_B, :]) + bias_ref[...] + madd
        p = jnp.exp(lg - jnp.max(lg, axis=-1, keepdims=True))
        p = (p * (1.0 / jnp.sum(p, axis=-1, keepdims=True))).astype(bf16)
        o_ref[...] = _dot_nt(p, kvt[HD_B:2 * HD_B, :]).astype(bf16)


def _dsa_sample(page_table, cache, layer, rbt, qb_s, qi_s, wi_s, own, ownt):
    nb, n_pages = page_table.shape
    pg = PAGES_PER_STEP
    past = n_pages * PAGE_SIZE
    sks = past + PAGE_SIZE
    topk = max(1, min(DSA_TOPK, (past + 1) // 4))
    row_w = cache.shape[2]
    in_specs = [pl.BlockSpec(rbt.shape, lambda bi, c, pt: (0, 0)),
                pl.BlockSpec((None, HB, HD_B), lambda bi, c, pt: (bi, 0, 0)),
                pl.BlockSpec((None, H_IDX, D_IDX), lambda bi, c, pt: (bi, 0, 0)),
                pl.BlockSpec((None, H_IDX, 1), lambda bi, c, pt: (bi, 0, 0)),
                pl.BlockSpec((None, 1, row_w), lambda bi, c, pt: (bi, 0, 0)),
                pl.BlockSpec((None, row_w, 1), lambda bi, c, pt: (bi, 0, 0))]
    in_specs += [pl.BlockSpec((None, None, row_w, PAGE_SIZE),
                              functools.partial(lambda bi, c, pt, kk: (layer, pt[bi, c * pg + kk], 0, 0), kk=kk))
                 for kk in range(pg)]
    grid_spec = pltpu.PrefetchScalarGridSpec(
        num_scalar_prefetch=1, grid=(nb, n_pages // pg), in_specs=in_specs,
        out_specs=pl.BlockSpec((None, HB, HD_B), lambda bi, c, pt: (bi, 0, 0)),
        scratch_shapes=[pltpu.VMEM((1, sks), f32), pltpu.VMEM((2 * HD_B, sks), bf16), pltpu.VMEM((HB, sks), f32)])
    out = pl.pallas_call(
        functools.partial(_dsa_s_kernel, pg=pg, past=past, topk=topk), grid_spec=grid_spec,
        out_shape=jax.ShapeDtypeStruct((nb, HB, HD_B), bf16), compiler_params=_cparams(2), name="dsa_sample")(
        page_table, rbt, qb_s, qi_s, wi_s, own, ownt, *([cache] * pg))
    return out.reshape(nb, HB * HD_B)


def _merge_kernel(oa_ref, ob_ref, wa_ref, wb_ref, ga_ref, gb_ref, o_ref):
    ga = 1.0 / (1.0 + jnp.exp(-ga_ref[...]))
    gb = 1.0 / (1.0 + jnp.exp(-gb_ref[...]))
    o_ref[...] = (ga * _dot(oa_ref[...], wa_ref[...]) + gb * _dot(ob_ref[...], wb_ref[...])).astype(bf16)


def _merge(oa, ob, wa, wb, zbig, d, tm, tn):
    n = oa.shape[0]
    nj = d // tn
    return pl.pallas_call(
        _merge_kernel, grid=(n // tm, nj),
        in_specs=[pl.BlockSpec((tm, oa.shape[1]), lambda i, j: (i, 0)),
                  pl.BlockSpec((tm, ob.shape[1]), lambda i, j: (i, 0)),
                  pl.BlockSpec((wa.shape[0], tn), lambda i, j: (0, j)),
                  pl.BlockSpec((wb.shape[0], tn), lambda i, j: (0, j)),
                  pl.BlockSpec((tm, tn), lambda i, j: (i, j)),
                  pl.BlockSpec((tm, tn), lambda i, j: (i, nj + j))],
        out_specs=pl.BlockSpec((tm, tn), lambda i, j: (i, j)),
        out_shape=jax.ShapeDtypeStruct((n, d), bf16), compiler_params=_cparams(2), name="merge")(
        oa, ob, wa, wb, zbig, zbig)


def _top_rows(s, k):
    r = s.shape[0]
    ridx = lax.broadcasted_iota(i32, s.shape, 0)
    vals, idxs = [], []
    for _ in range(k):
        m = jnp.max(s, axis=0, keepdims=True)
        am = jnp.min(jnp.where(s == m, ridx, r), axis=0, keepdims=True)
        vals.append(m)
        idxs.append(am)
        s = jnp.where(ridx == am, NEG_INF, s)
    return jnp.concatenate(vals, axis=0), jnp.concatenate(idxs, axis=0)


def _pick_rows(tab, sel):
    out = jnp.zeros(sel.shape, tab.dtype)
    for i in range(tab.shape[0]):
        out = jnp.where(sel == i, tab[i:i + 1, :], out)
    return out


def _peer_sel_kernel(q_ref, k1_ref, k2_ref, a_ref, b_ref, g_ref):
    half = PEER_QDIM // 2
    shift = PEER_TOPK.bit_length() - 1
    assert 1 << shift == PEER_TOPK
    k1 = k1_ref[...]
    k2 = k2_ref[...]
    a_rows, b_rows, g_rows = [], [], []
    for h in range(PEER_HEADS):
        q1 = q_ref[:, h * PEER_QDIM:h * PEER_QDIM + half]
        q2 = q_ref[:, h * PEER_QDIM + half:(h + 1) * PEER_QDIM]
        v1, i1 = _top_rows(_dot_nt(k1, q1), PEER_TOPK)
        v2, i2 = _top_rows(_dot_nt(k2, q2), PEER_TOPK)
        cand = jnp.concatenate([v1[i:i + 1, :] + v2 for i in range(PEER_TOPK)], axis=0)
        best, pos = _top_rows(cand, PEER_TOPK)
        a_rows.append(_pick_rows(i1, pos >> shift))
        b_rows.append(_pick_rows(i2, pos & (PEER_TOPK - 1)))
        e = jnp.exp(best - best[0:1, :])
        g_rows.append(e / jnp.sum(e, axis=0, keepdims=True))
    a_ref[...] = jnp.concatenate(a_rows, axis=0).astype(f32).T.astype(i32)
    b_ref[...] = jnp.concatenate(b_rows, axis=0).astype(f32).T.astype(i32)
    g_ref[...] = jnp.concatenate(g_rows, axis=0).T


def _peer_select(q, k1, k2, ts):
    n = q.shape[0]
    slots = PEER_HEADS * PEER_TOPK
    row = pl.BlockSpec((ts, slots), lambda i: (i, 0))
    return pl.pallas_call(
        _peer_sel_kernel, grid=(n // ts,),
        in_specs=[pl.BlockSpec((ts, q.shape[1]), lambda i: (i, 0)),
                  pl.BlockSpec(k1.shape, lambda i: (0, 0)), pl.BlockSpec(k2.shape, lambda i: (0, 0))],
        out_specs=[row, row, row],
        out_shape=[jax.ShapeDtypeStruct((n, slots), i32), jax.ShapeDtypeStruct((n, slots), i32),
                   jax.ShapeDtypeStruct((n, slots), f32)],
        compiler_params=_cparams(1), name="peer_select")(q, k1, k2)


def _peer_gate_kernel(a_ref, b_ref, g_ref, o_ref):
    ts, slots = a_ref.shape
    krow = lax.broadcasted_iota(i32, (N_KEYS, slots), 0)

    def body(n, carry):
        a = jnp.broadcast_to(a_ref[pl.ds(n, 1), :], (N_KEYS, slots))
        b = jnp.broadcast_to(b_ref[pl.ds(n, 1), :], (N_KEYS, slots))
        g = jnp.broadcast_to(g_ref[pl.ds(n, 1), :], (N_KEYS, slots))
        ma = jnp.where(a == krow, g, 0.0).astype(bf16)
        mb = jnp.where(b == krow, 1.0, 0.0).astype(bf16)
        o_ref[n] = _dot_nt(ma, mb).astype(bf16)
        return carry

    lax.fori_loop(0, ts, body, 0, unroll=8)


def _peer_gates(a, b, g, ts):
    n, slots = a.shape
    row = pl.BlockSpec((ts, slots), lambda i: (i, 0))
    return pl.pallas_call(
        _peer_gate_kernel, grid=(n // ts,), in_specs=[row, row, row],
        out_specs=pl.BlockSpec((ts, N_KEYS, N_KEYS), lambda i: (i, 0, 0)),
        out_shape=jax.ShapeDtypeStruct((n, N_KEYS, N_KEYS), bf16), compiler_params=_cparams(1), name="peer_gates")(a, b, g)


def _peer_dense_kernel(xn_ref, gd_ref, u_ref, v_ref, o_ref):
    @pl.when(pl.program_id(1) == 0)
    def _():
        o_ref[...] = jnp.zeros_like(o_ref)

    hid = _dot_nt(xn_ref[...], u_ref[...])
    act = gd_ref[...].astype(f32) * (0.5 * hid * (1.0 + lax.erf(hid * (2.0 ** -0.5))))
    o_ref[...] += _dot(act.astype(bf16), v_ref[...])


def _peer_dense(xn, gd, u, v, tm, te):
    n, d = xn.shape
    ne = u.shape[0]
    return pl.pallas_call(
        _peer_dense_kernel, grid=(n // tm, ne // te),
        in_specs=[pl.BlockSpec((tm, d), lambda i, e: (i, 0)), pl.BlockSpec((tm, te), lambda i, e: (i, e)),
                  pl.BlockSpec((te, d), lambda i, e: (e, 0)), pl.BlockSpec((te, d), lambda i, e: (e, 0))],
        out_specs=pl.BlockSpec((tm, d), lambda i, e: (i, 0)),
        out_shape=jax.ShapeDtypeStruct((n, d), f32), compiler_params=_cparams(2), name="peer_dense")(xn, gd, u, v)


def _prep_layer(l, norm1_g, w_in, cq_norm_g, ckv_norm_g, w_uq, w_uk, w_uv, q_norm_a, k_norm_a,
                q_norm_b, k_norm_b, w_up_a, w_up_b, w_out, norm2_g, peer_wq, peer_k1, peer_k2, peer_u, peer_v):
    d = w_in.shape[1]
    win = w_in[l]
    splits = (Q_LORA, KV_LORA, D_ROPE, HB * HD_B, HD_B, HD_B, H_IDX * D_IDX, D_IDX, H_IDX, d, d)
    offs = [0]
    for s in splits:
        offs.append(offs[-1] + s)
    cq, ckv, kpe, qb, kb, vb, qi, ki, wi, ga, gb = [win[:, offs[i]:offs[i + 1]] for i in range(len(splits))]

    def zeros(w):
        return jnp.zeros((d, w), win.dtype)

    small = [cq, ckv, kpe, zeros(LANES - D_ROPE), kb, vb, ki, wi, zeros(LANES - D_IDX - H_IDX)]
    zs_w = sum(a.shape[1] for a in small)
    pad = (-zs_w) % 512
    w_s = jnp.concatenate(small + ([zeros(pad)] if pad else []), axis=1).astype(bf16)
    w_big = jnp.concatenate([ga, gb, qb], axis=1).astype(bf16)
    wuq3 = w_uq[l].reshape(Q_LORA, HA, D_QK_A)
    wuq = jnp.concatenate([wuq3[:, :, :D_NOPE].reshape(Q_LORA, HA * D_NOPE),
                           wuq3[:, :, D_NOPE:].reshape(Q_LORA, HA * D_ROPE)], axis=1).astype(bf16)
    ones_half = jnp.ones((LANES - HD_B,), f32)
    return dict(
        norm1_g=norm1_g[l], norm2_g=norm2_g[l], w_s=w_s, w_big=w_big, w_qi=qi.astype(bf16),
        wuq=wuq, wukt=jnp.transpose(w_uk[l], (1, 2, 0)).astype(bf16),
        wuk2=w_uk[l].reshape(KV_LORA, HA * D_NOPE).astype(bf16),
        wuv=jnp.transpose(w_uv[l], (1, 0, 2)).astype(bf16), wuvf=w_uv[l].reshape(KV_LORA, HA * D_VA).astype(bf16),
        g_cq=cq_norm_g[l].reshape(1, Q_LORA), g_ckv=ckv_norm_g[l].reshape(1, KV_LORA),
        g_qan=q_norm_a[l][:D_NOPE].reshape(1, D_NOPE), g_qap=jnp.tile(q_norm_a[l][D_NOPE:], 2).reshape(1, LANES),
        g_kan=k_norm_a[l][:D_NOPE].reshape(1, D_NOPE),
        g_kap=jnp.concatenate([k_norm_a[l][D_NOPE:], jnp.zeros((LANES - D_ROPE,), f32)]).reshape(1, LANES),
        g_qb=jnp.tile(q_norm_b[l], LANES // HD_B).reshape(1, LANES),
        g_kb=jnp.concatenate([k_norm_b[l], ones_half]).reshape(1, LANES),
        w_up_a=w_up_a[l].astype(bf16), w_up_b=w_up_b[l].astype(bf16), w_out=w_out[l].astype(bf16),
        peer_wq=peer_wq[l].astype(bf16), peer_k1=peer_k1[l].astype(bf16), peer_k2=peer_k2[l].astype(bf16),
        peer_u=peer_u[l].astype(bf16), peer_v=peer_v[l].astype(bf16))


def _rope_tables(pos):
    half = D_ROPE // 2
    inv_freq = ROPE_THETA ** (-jnp.arange(half, dtype=f32) / half)
    ang = pos.astype(f32)[:, None] * inv_freq[None, :]
    cos, sin = jnp.cos(ang), jnp.sin(ang)
    return jnp.tile(jnp.concatenate([cos, cos], axis=1), (1, 2)), jnp.tile(jnp.concatenate([-sin, sin], axis=1), (1, 2))


def kernel(x_prompt, x_sample, cache_mla, cache_mla_kinv, cache_dsa, page_table, rel_bias, norm1_g, w_in, cq_norm_g, ckv_norm_g, w_uq, w_uk, w_uv, q_norm_a, k_norm_a, q_norm_b, k_norm_b, w_up_a, w_up_b, w_out, norm2_g, peer_wq, peer_k1, peer_k2, peer_u, peer_v):
    b, t, d = x_prompt.shape
    nb, ts_, _ = x_sample.shape
    depth = w_in.shape[0]
    assert ts_ == 1 and HA % 2 == 0 and Q_LORA % LANES == 0 and KV_LORA % LANES == 0
    assert cache_mla.shape[2] == PAGE_SIZE and N_KEYS == LANES and PEER_HEADS * PEER_TOPK == LANES
    npr = b * t
    n = npr + nb
    past = page_table.shape[1] * PAGE_SIZE

    tm_big = _row_tile(n, 832)
    tm_feat = _row_tile(n, 208)
    tm_small = _row_tile(n, 208)
    tm_ffn = _row_tile(n, 640)
    tq_a = _row_tile(t, 256)
    tq_b = LANES
    assert t % tq_b == 0 and n % LANES == 0

    pos = jnp.concatenate([jnp.tile(jnp.arange(t, dtype=i32), b), jnp.full((nb,), past, i32)])
    cosq, sinq = _rope_tables(pos)
    rbt = rel_bias.T
    cache_mla_t = jnp.swapaxes(cache_mla, 2, 3)
    cache_kinv_t = jnp.swapaxes(cache_mla_kinv, 2, 3)
    cache_dsa_t = jnp.swapaxes(cache_dsa, 2, 3)

    h_prev = jnp.concatenate([x_prompt.reshape(npr, d), x_sample.reshape(nb, d)], axis=0)
    ffn_prev = None
    rows = {k: [] for k in ("mla", "kinv", "dsa")}
    for l in range(depth):
        wl = _prep_layer(l, norm1_g, w_in, cq_norm_g, ckv_norm_g, w_uq, w_uk, w_uv, q_norm_a, k_norm_a,
                         q_norm_b, k_norm_b, w_up_a, w_up_b, w_out, norm2_g, peer_wq, peer_k1, peer_k2, peer_u, peer_v)
        x, xn = _addnorm(h_prev, ffn_prev, wl["norm1_g"], tm_small)
        if x is None:
            x = h_prev
        zs = _mm(xn, wl["w_s"], f32, tm_big, _col_tile(wl["w_s"].shape[1], 512))
        zbig = _mm(xn, wl["w_big"], f32, tm_big, _col_tile(math.gcd(d, wl["w_big"].shape[1]), 512))
        qi = _mm(xn, wl["w_qi"], bf16, tm_big, _col_tile(wl["w_qi"].shape[1], 512))
        qlat, qpe, mla_row, mlab, kinv, qbn, dsa_row, dsab, wi = _features(zs, zbig, cosq, sinq, wl, tm_feat)
        rows["mla"].append(mla_row)
        rows["kinv"].append(kinv)
        rows["dsa"].append(dsa_row)

        oa_p = _mla_prompt(qlat, qpe, mlab, kinv.T, wl["wuv"], b, t, tq_a)
        oa_s = _mla_sample(page_table, cache_mla_t, cache_kinv_t, l,
                           jnp.transpose(qlat[:, npr:, :], (1, 0, 2)), qpe[npr:].reshape(nb, HA, D_ROPE),
                           mlab[npr:].reshape(nb, 1, KV_LORA + D_ROPE), kinv[npr:].reshape(nb, HA, 1), wl["wuvf"])
        ob_p = _dsa_prompt(rel_bias, qbn, qi, wi, dsab, b, t, tq_b)
        ob_s = _dsa_sample(page_table, cache_dsa_t, l, rbt, qbn[npr:].reshape(nb, HB, HD_B),
                           qi[npr:].reshape(nb, H_IDX, D_IDX), wi[npr:].reshape(nb, H_IDX, 1),
                           dsab[npr:].reshape(nb, 1, 2 * HD_B + D_IDX), dsab[npr:].reshape(nb, 2 * HD_B + D_IDX, 1))
        oa = jnp.concatenate([oa_p, oa_s], axis=0)
        ob = jnp.concatenate([ob_p, ob_s], axis=0)
        tn_d = _col_tile(d, 512)
        mix = _merge(oa, ob, wl["w_up_a"], wl["w_up_b"], zbig, d, tm_big, tn_d)
        h = _mm(mix, wl["w_out"], f32, tm_big, tn_d, res=x)

        _, xn2 = _addnorm(h, None, wl["norm2_g"], tm_small)
        pq = _mm(xn2, wl["peer_wq"], bf16, tm_big, _col_tile(wl["peer_wq"].shape[1], 512))
        ea, eb, eg = _peer_select(pq, wl["peer_k1"], wl["peer_k2"], LANES)
        gd = _peer_gates(ea, eb, eg, LANES).reshape(n, N_KEYS * N_KEYS)
        ffn_prev = _peer_dense(xn2, gd, wl["peer_u"], wl["peer_v"], tm_ffn, 512)
        h_prev = h

    y = _add(h_prev, ffn_prev, tm_small)

    def stack(key, lo, hi, shape):
        return jnp.stack([r[lo:hi].reshape(shape + (r.shape[1],)) for r in rows[key]])

    return (y[:npr].reshape(b, t, d), y[npr:].reshape(nb, 1, d),
            stack("mla", 0, npr, (b, t)), stack("kinv", 0, npr, (b, t)), stack("dsa", 0, npr, (b, t)),
            stack("mla", npr, n, (nb, 1)), stack("kinv", npr, n, (nb, 1)), stack("dsa", npr, n, (nb, 1)))
```

```python
import functools
import math

import jax
import jax.numpy as jnp
from jax import lax
from jax.experimental import pallas as pl
from jax.experimental.pallas import tpu as pltpu

f32 = jnp.float32
bf16 = jnp.bfloat16
i32 = jnp.int32

HA = 16
D_NOPE = 128
D_ROPE = 64
D_QK_A = D_NOPE + D_ROPE
D_VA = 128
Q_LORA = 768
KV_LORA = 256
ROPE_THETA = 10000.0
HB = 32
HD_B = 64
H_IDX = 32
D_IDX = 64
DSA_TOPK = 256
N_BUCKETS = 32
MAX_DIST = 128
PEER_HEADS = 8
PEER_QDIM = 256
N_KEYS = 128
PEER_TOPK = 16
PAGE_SIZE = 128
EPS = 1e-6

LANES = 128
ROW_ALIGN = 16
VMEM_LIMIT = 56 * 1024 * 1024
PAGES_PER_STEP = 16
DSA_GROUP = 2
NEG_INF = float("-inf")


def _cparams(n_axes):
    return pltpu.CompilerParams(dimension_semantics=("arbitrary",) * n_axes, vmem_limit_bytes=VMEM_LIMIT)


def _row_tile(n, cap):
    best = None
    for t in range(ROW_ALIGN, min(n, cap) + 1, ROW_ALIGN):
        if n % t == 0:
            best = t
    assert best is not None, (n, cap)
    return best


def _col_tile(m, cap):
    best = None
    for t in range(LANES, min(m, cap) + 1, LANES):
        if m % t == 0:
            best = t
    assert best is not None, (m, cap)
    return best


def _dot(a, b):
    return jnp.dot(a, b, preferred_element_type=f32)


def _dot_nt(a, b):
    return lax.dot_general(a, b, (((1,), (1,)), ((), ())), preferred_element_type=f32)


def _transpose_exact(x):
    c = x.shape[1]
    eye = (lax.broadcasted_iota(i32, (c, c), 0) == lax.broadcasted_iota(i32, (c, c), 1)).astype(bf16)
    hi = x.astype(bf16)
    r1 = x - hi.astype(f32)
    mid = r1.astype(bf16)
    lo = (r1 - mid.astype(f32)).astype(bf16)
    return _dot_nt(eye, hi) + _dot_nt(eye, mid) + _dot_nt(eye, lo)


def _t5_bucket(d):
    n = jnp.maximum(d, 0)
    max_exact = N_BUCKETS // 2
    nf = jnp.maximum(n, 1).astype(f32)
    large = max_exact + (jnp.log(nf / max_exact) / math.log(MAX_DIST / max_exact)
                         * (N_BUCKETS - max_exact)).astype(i32)
    large = jnp.minimum(large, N_BUCKETS - 1)
    return jnp.where(n < max_exact, n, large)


def _float_key(x):
    u = pltpu.bitcast(x, i32)
    return u ^ ((u >> 31) & jnp.int32(0x7FFFFFFF))


def _topk_mask(sc, kpos, valid, k, idx_bits):
    key = _float_key(sc)
    kf = float(k)
    r = sc.shape[0]
    int_min = jnp.iinfo(jnp.int32).min

    def count_ge(t):
        return jnp.sum((key >= t).astype(f32), axis=1, keepdims=True)

    t0 = jnp.full((r, 1), int_min, i32)
    t0 = jnp.where(count_ge(jnp.zeros_like(t0)) >= kf, jnp.zeros_like(t0), t0)

    def tbody(it, t):
        cand = t | (jnp.int32(1) << (30 - it))
        return jnp.where(count_ge(cand) >= kf, cand, t)

    t = lax.fori_loop(0, 31, tbody, t0)
    gt = key > t
    eq = key == t
    n_gt = jnp.sum(gt.astype(f32), axis=1, keepdims=True)

    def jbody(it, j):
        bit = idx_bits - 1 - it
        cand = j | ((jnp.int32(1) << bit) - 1)
        cnt = n_gt + jnp.sum((eq & (kpos <= cand)).astype(f32), axis=1, keepdims=True)
        return jnp.where(cnt >= kf, j, j | (jnp.int32(1) << bit))

    j = lax.fori_loop(0, idx_bits, jbody, jnp.zeros((r, 1), i32))
    return valid & (gt | (eq & (kpos <= j)))


def _topk_mask_row(sc, kpos, valid, k, idx_bits):
    radix_bits = 3
    nc = 1 << radix_bits
    c = sc.shape[1]
    kf = float(k)
    key = jnp.broadcast_to(_float_key(sc), (nc, c))
    j = lax.broadcasted_iota(i32, (nc, 1), 0)

    def count(mask):
        return jnp.sum(mask.astype(f32), axis=1, keepdims=True)

    def best(ok, jj):
        return jnp.max(jnp.where(ok, jj, 0), axis=0, keepdims=True)

    nonneg = count(key >= 0)[0:1, :] >= kf
    t = jnp.where(nonneg, 0, jnp.iinfo(jnp.int32).min).astype(i32)
    for shift in range(31 - radix_bits, -1, -radix_bits):
        t = t + (best(count(key >= t + (j << shift)) >= kf, j) << shift)
    j1 = jnp.minimum(j, 1)
    t = t + best(count(key >= t + j1) >= kf, j1)

    gt = key > t
    eq = key == t
    n_gt = count(gt)
    kposb = jnp.broadcast_to(kpos, (nc, c))

    def below(cands):
        return n_gt + count(eq & (kposb <= cands)) < kf

    m = jnp.zeros((1, 1), i32)
    nsteps = -(-idx_bits // radix_bits)
    for s in range(nsteps):
        shift = radix_bits * (nsteps - 1 - s)
        m = m + (best(below(m + (j << shift)), j) << shift)
    jstar = jnp.where(below(jnp.broadcast_to(m, (nc, 1)))[0:1, :], m + 1, m)
    return valid & (gt[0:1, :] | (eq[0:1, :] & (kpos <= jstar)))


def _addnorm_kernel(*refs, has_res):
    if has_res:
        h_ref, r_ref, g_ref, x_ref, xn_ref = refs
        x = h_ref[...] + r_ref[...]
        x_ref[...] = x
    else:
        h_ref, g_ref, xn_ref = refs
        x = h_ref[...]
    y = x * lax.rsqrt(jnp.mean(x * x, axis=-1, keepdims=True) + EPS)
    xn_ref[...] = (y * g_ref[...]).astype(bf16)


def _addnorm(h, res, g, tm):
    n, d = h.shape
    row = pl.BlockSpec((tm, d), lambda i: (i, 0))
    gspec = pl.BlockSpec((1, d), lambda i: (0, 0))
    g2 = g.reshape(1, d)
    if res is None:
        xn = pl.pallas_call(
            functools.partial(_addnorm_kernel, has_res=False), grid=(n // tm,),
            in_specs=[row, gspec], out_specs=row,
            out_shape=jax.ShapeDtypeStruct((n, d), bf16), compiler_params=_cparams(1), name="norm")(h, g2)
        return None, xn
    x, xn = pl.pallas_call(
        functools.partial(_addnorm_kernel, has_res=True), grid=(n // tm,),
        in_specs=[row, row, gspec], out_specs=[row, row],
        out_shape=[jax.ShapeDtypeStruct((n, d), f32), jax.ShapeDtypeStruct((n, d), bf16)],
        compiler_params=_cparams(1), name="addnorm")(h, res, g2)
    return x, xn


def _add_kernel(a_ref, b_ref, o_ref):
    o_ref[...] = a_ref[...] + b_ref[...]


def _add(a, b, tm):
    n, d = a.shape
    row = pl.BlockSpec((tm, d), lambda i: (i, 0))
    return pl.pallas_call(_add_kernel, grid=(n // tm,), in_specs=[row, row], out_specs=row,
                          out_shape=jax.ShapeDtypeStruct((n, d), f32), compiler_params=_cparams(1), name="add")(a, b)


def _mm_kernel(*refs, has_res):
    if has_res:
        x_ref, w_ref, r_ref, o_ref = refs
        o_ref[...] = (r_ref[...] + _dot(x_ref[...], w_ref[...])).astype(o_ref.dtype)
    else:
        x_ref, w_ref, o_ref = refs
        o_ref[...] = _dot(x_ref[...], w_ref[...]).astype(o_ref.dtype)


def _mm(x, w, out_dtype, tm, tn, res=None):
    n, k = x.shape
    m = w.shape[1]
    in_specs = [pl.BlockSpec((tm, k), lambda i, j: (i, 0)), pl.BlockSpec((k, tn), lambda i, j: (0, j))]
    args = [x, w]
    if res is not None:
        in_specs.append(pl.BlockSpec((tm, tn), lambda i, j: (i, j)))
        args.append(res)
    return pl.pallas_call(
        functools.partial(_mm_kernel, has_res=res is not None), grid=(n // tm, m // tn),
        in_specs=in_specs, out_specs=pl.BlockSpec((tm, tn), lambda i, j: (i, j)),
        out_shape=jax.ShapeDtypeStruct((n, m), out_dtype), compiler_params=_cparams(2), name="matmul")(*args)


def _feat_kernel(zs_ref, zq_ref, cos_ref, sin_ref, wuq_ref, wukt_ref, wuk2_ref,
                 gcq_ref, gckv_ref, gqan_ref, gqap_ref, gkan_ref, gkap_ref, gqb_ref, gkb_ref,
                 qlat_ref, qpe_ref, mla_ref, mlab_ref, kinv_ref, qbn_ref, dsa_ref, dsab_ref, wi_ref):
    tm = zs_ref.shape[0]
    lane = lax.broadcasted_iota(i32, (tm, LANES), 1)
    lo = lane < 64
    cosq = cos_ref[...]
    sinq = sin_ref[...]

    def rot_half(x):
        return jnp.where((lane & 32) == 0, pltpu.roll(x, 96, 1), pltpu.roll(x, 32, 1))

    def rsum(x):
        return jnp.sum(x, axis=-1, keepdims=True)

    zcq = zs_ref[:, 0:Q_LORA]
    cq = zcq * lax.rsqrt(jnp.mean(zcq * zcq, axis=-1, keepdims=True) + EPS) * gcq_ref[...]
    q = _dot(cq.astype(bf16), wuq_ref[...])
    zckv = zs_ref[:, Q_LORA:Q_LORA + KV_LORA]
    ckv = zckv * lax.rsqrt(jnp.mean(zckv * zckv, axis=-1, keepdims=True) + EPS) * gckv_ref[...]
    o_kpe = Q_LORA + KV_LORA
    kpe_raw = zs_ref[:, o_kpe:o_kpe + LANES]
    knope = _dot(ckv.astype(bf16), wuk2_ref[...])
    kpe_sq = rsum(kpe_raw * kpe_raw)
    lane_h = lax.broadcasted_iota(i32, (tm, HA), 1)
    ksq = jnp.zeros((tm, HA), f32)
    for h in range(HA):
        kn = knope[:, h * D_NOPE:(h + 1) * D_NOPE]
        ksq = jnp.where(lane_h == h, rsum(kn * kn) + kpe_sq, ksq)
    kinv_ref[...] = lax.rsqrt(ksq / D_QK_A + EPS)
    kpe = kpe_raw * gkap_ref[...]
    kpe = kpe * cosq + rot_half(kpe) * sinq
    mla_ref[:, 0:KV_LORA] = ckv
    mla_ref[:, KV_LORA:KV_LORA + D_ROPE] = kpe[:, 0:D_ROPE]
    mlab_ref[:, 0:KV_LORA] = ckv.astype(bf16)
    mlab_ref[:, KV_LORA:KV_LORA + D_ROPE] = kpe[:, 0:D_ROPE].astype(bf16)

    gqan = gqan_ref[...]
    gqap = gqap_ref[...]
    gkan = gkan_ref[...]
    for p in range(HA // 2):
        xp = q[:, HA * D_NOPE + p * LANES:HA * D_NOPE + (p + 1) * LANES]
        sq = xp * xp
        s_pair = (rsum(jnp.where(lo, sq, 0.0)), rsum(jnp.where(lo, 0.0, sq)))
        invs = []
        for e in range(2):
            hh = 2 * p + e
            xn = q[:, hh * D_NOPE:(hh + 1) * D_NOPE]
            inv = lax.rsqrt((rsum(xn * xn) + s_pair[e]) / D_QK_A + EPS)
            invs.append(inv)
            qa = xn * inv * gqan
            ql = _dot((qa * gkan).astype(bf16), wukt_ref[hh])
            qlat_ref[hh] = ql.astype(bf16)
        xa = xp * jnp.where(lo, invs[0], invs[1]) * gqap
        qpe_ref[:, p * LANES:(p + 1) * LANES] = (xa * cosq + rot_half(xa) * sinq).astype(bf16)

    gqb = gqb_ref[...]
    scale_b = HD_B ** -0.5
    for p in range(HB * HD_B // LANES):
        x = zq_ref[:, p * LANES:(p + 1) * LANES]
        sq = x * x
        inv = jnp.where(lo, lax.rsqrt(rsum(jnp.where(lo, sq, 0.0)) / HD_B + EPS),
                        lax.rsqrt(rsum(jnp.where(lo, 0.0, sq)) / HD_B + EPS))
        qbn_ref[:, p * LANES:(p + 1) * LANES] = (x * inv * gqb * scale_b).astype(bf16)
    o_c = o_kpe + LANES
    t1 = zs_ref[:, o_c:o_c + LANES]
    invk = lax.rsqrt(rsum(jnp.where(lo, t1 * t1, 0.0)) / HD_B + EPS)
    t1 = jnp.where(lo, t1 * invk * gkb_ref[...], t1)
    t2 = zs_ref[:, o_c + LANES:o_c + 2 * LANES]
    dsa_ref[:, 0:LANES] = t1
    dsa_ref[:, LANES:LANES + D_IDX] = t2[:, 0:D_IDX]
    dsab_ref[:, 0:LANES] = t1.astype(bf16)
    dsab_ref[:, LANES:LANES + D_IDX] = t2[:, 0:D_IDX].astype(bf16)
    wi_ref[...] = t2[:, D_IDX:D_IDX + H_IDX] * (H_IDX * D_IDX) ** -0.5


def _features(zs, zbig, cosq, sinq, wl, tm):
    n = zs.shape[0]
    zs_w = zs.shape[1]
    qb_w = HB * HD_B
    qb_blk = (zbig.shape[1] - qb_w) // qb_w

    def row(w):
        return pl.BlockSpec((tm, w), lambda i: (i, 0))

    def full(a):
        return pl.BlockSpec(a.shape, lambda i: (0,) * a.ndim)

    gains = [wl["g_cq"], wl["g_ckv"], wl["g_qan"], wl["g_qap"], wl["g_kan"], wl["g_kap"], wl["g_qb"], wl["g_kb"]]
    in_specs = [row(zs_w), pl.BlockSpec((tm, qb_w), lambda i: (i, qb_blk)), row(LANES), row(LANES),
                full(wl["wuq"]), full(wl["wukt"]), full(wl["wuk2"])] + [full(g) for g in gains]
    out_specs = [pl.BlockSpec((HA, tm, KV_LORA), lambda i: (0, i, 0)), row(HA * D_ROPE),
                 row(KV_LORA + D_ROPE), row(KV_LORA + D_ROPE), row(HA), row(qb_w),
                 row(2 * HD_B + D_IDX), row(2 * HD_B + D_IDX), row(H_IDX)]
    out_shape = [jax.ShapeDtypeStruct((HA, n, KV_LORA), bf16), jax.ShapeDtypeStruct((n, HA * D_ROPE), bf16),
                 jax.ShapeDtypeStruct((n, KV_LORA + D_ROPE), f32), jax.ShapeDtypeStruct((n, KV_LORA + D_ROPE), bf16),
                 jax.ShapeDtypeStruct((n, HA), f32), jax.ShapeDtypeStruct((n, qb_w), bf16),
                 jax.ShapeDtypeStruct((n, 2 * HD_B + D_IDX), f32), jax.ShapeDtypeStruct((n, 2 * HD_B + D_IDX), bf16),
                 jax.ShapeDtypeStruct((n, H_IDX), f32)]
    return pl.pallas_call(_feat_kernel, grid=(n // tm,), in_specs=in_specs, out_specs=out_specs,
                          out_shape=out_shape, compiler_params=_cparams(1), name="features")(
        zs, zbig, cosq, sinq, wl["wuq"], wl["wukt"], wl["wuk2"], *gains)


def _mla_p_kernel(qlat_ref, qpe_ref, k_ref, kinvt_ref, wuv_ref, oa_ref, *, i):
    tq = qpe_ref.shape[0]
    s_len = (i + 1) * tq
    k = k_ref[0:s_len, :]
    ckv = k[:, 0:KV_LORA]
    kpe = k[:, KV_LORA:KV_LORA + D_ROPE]
    kinv = kinvt_ref[:, 0:s_len] * (D_QK_A ** -0.5)
    qpos = i * tq + lax.broadcasted_iota(i32, (tq, s_len), 0)
    valid = lax.broadcasted_iota(i32, (tq, s_len), 1) <= qpos
    for h in range(HA):
        s = _dot_nt(qlat_ref[h], ckv) + _dot_nt(qpe_ref[:, h * D_ROPE:(h + 1) * D_ROPE], kpe)
        s = jnp.where(valid, s * kinv[h:h + 1, :], NEG_INF)
        p = jnp.exp(s - jnp.max(s, axis=-1, keepdims=True))
        ctx = _dot(p.astype(bf16), ckv) * (1.0 / jnp.sum(p, axis=-1, keepdims=True))
        oa_ref[:, h * D_VA:(h + 1) * D_VA] = _dot(ctx.astype(bf16), wuv_ref[h]).astype(bf16)


def _mla_prompt(qlat, qpe, mlab, kinvt, wuv, b, t, tq):
    nq = t // tq
    outs = []
    for i in range(nq):
        outs.append(pl.pallas_call(
            functools.partial(_mla_p_kernel, i=i), grid=(b,),
            in_specs=[pl.BlockSpec((HA, tq, KV_LORA), lambda bi, i=i: (0, bi * nq + i, 0)),
                      pl.BlockSpec((tq, HA * D_ROPE), lambda bi, i=i: (bi * nq + i, 0)),
                      pl.BlockSpec((t, KV_LORA + D_ROPE), lambda bi: (bi, 0)),
                      pl.BlockSpec((HA, t), lambda bi: (0, bi)),
                      pl.BlockSpec(wuv.shape, lambda bi: (0, 0, 0))],
            out_specs=pl.BlockSpec((None, tq, HA * D_VA), lambda bi: (bi, 0, 0)),
            out_shape=jax.ShapeDtypeStruct((b, tq, HA * D_VA), bf16), compiler_params=_cparams(1),
            name="mla_prompt")(qlat, qpe, mlab, kinvt, wuv))
    return jnp.concatenate(outs, axis=1).reshape(b * t, HA * D_VA)


def _mla_s_kernel(pt_ref, ql_ref, qp_ref, kown_ref, kinvown_ref, wuvf_ref, *rest, pg):
    pages = rest[:pg]
    kinvp = rest[pg:2 * pg]
    o_ref = rest[2 * pg]
    m_ref, l_ref, acc_ref = rest[2 * pg + 1:]
    c = pl.program_id(1)
    scale = D_QK_A ** -0.5
    ql = ql_ref[...]
    qp = qp_ref[...]

    @pl.when(c == 0)
    def _():
        ko = kown_ref[...].astype(f32)
        s0 = (jnp.sum(ql.astype(f32) * ko[:, 0:KV_LORA], axis=-1, keepdims=True)
              + jnp.sum(qp.astype(f32) * ko[:, KV_LORA:KV_LORA + D_ROPE], axis=-1, keepdims=True))
        s0 = s0 * (kinvown_ref[...] * scale)
        m_ref[...] = s0
        l_ref[...] = jnp.ones_like(s0)
        acc_ref[...] = jnp.broadcast_to(ko[:, 0:KV_LORA], acc_ref.shape)

    kt = jnp.concatenate([r[...] for r in pages], axis=1).astype(bf16)
    kinvt = jnp.concatenate([r[...] for r in kinvp], axis=1)
    s = (_dot(ql, kt[0:KV_LORA, :]) + _dot(qp, kt[KV_LORA:KV_LORA + D_ROPE, :])) * (kinvt * scale)
    m_old = m_ref[...]
    m_new = jnp.maximum(m_old, jnp.max(s, axis=-1, keepdims=True))
    alpha = jnp.exp(m_old - m_new)
    p = jnp.exp(s - m_new)
    l_ref[...] = alpha * l_ref[...] + jnp.sum(p, axis=-1, keepdims=True)
    acc_ref[...] = alpha * acc_ref[...] + _dot_nt(p.astype(bf16), kt[0:KV_LORA, :])
    m_ref[...] = m_new

    @pl.when(c == pl.num_programs(1) - 1)
    def _():
        ctx = acc_ref[...] / l_ref[...]
        full = _dot(ctx.astype(bf16), wuvf_ref[...])
        rowi = lax.broadcasted_iota(i32, full.shape, 0)
        coli = lax.broadcasted_iota(i32, full.shape, 1)
        own = (coli >= rowi * D_VA) & (coli < (rowi + 1) * D_VA)
        o_ref[...] = jnp.sum(jnp.where(own, full, 0.0), axis=0, keepdims=True).astype(bf16)


def _mla_sample(page_table, cache, cache_kinv, layer, ql_s, qp_s, kown, kinvown, wuvf):
    nb, n_pages = page_table.shape
    pg = PAGES_PER_STEP
    assert n_pages % pg == 0
    row_w = cache.shape[2]

    def page_spec(width, kk):
        return pl.BlockSpec((None, None, width, PAGE_SIZE), lambda bi, c, pt: (layer, pt[bi, c * pg + kk], 0, 0))

    in_specs = [pl.BlockSpec((None, HA, KV_LORA), lambda bi, c, pt: (bi, 0, 0)),
                pl.BlockSpec((None, HA, D_ROPE), lambda bi, c, pt: (bi, 0, 0)),
                pl.BlockSpec((None, 1, row_w), lambda bi, c, pt: (bi, 0, 0)),
                pl.BlockSpec((None, HA, 1), lambda bi, c, pt: (bi, 0, 0)),
                pl.BlockSpec(wuvf.shape, lambda bi, c, pt: (0, 0))]
    in_specs += [page_spec(row_w, kk) for kk in range(pg)] + [page_spec(HA, kk) for kk in range(pg)]
    grid_spec = pltpu.PrefetchScalarGridSpec(
        num_scalar_prefetch=1, grid=(nb, n_pages // pg), in_specs=in_specs,
        out_specs=pl.BlockSpec((None, 1, HA * D_VA), lambda bi, c, pt: (bi, 0, 0)),
        scratch_shapes=[pltpu.VMEM((HA, 1), f32), pltpu.VMEM((HA, 1), f32), pltpu.VMEM((HA, KV_LORA), f32)])
    out = pl.pallas_call(
        functools.partial(_mla_s_kernel, pg=pg), grid_spec=grid_spec,
        out_shape=jax.ShapeDtypeStruct((nb, 1, HA * D_VA), bf16), compiler_params=_cparams(2), name="mla_sample")(
        page_table, ql_s, qp_s, kown, kinvown, wuvf, *([cache] * pg), *([cache_kinv] * pg))
    return out.reshape(nb, HA * D_VA)


def _dsa_p_kernel(rb_ref, qb_ref, qi_ref, wi_ref, kfar_ref, kn0_ref, kn1_ref, ob_ref, toep_ref, *, topk, i0, far_len):
    tq = qb_ref.shape[0]
    t_len = kfar_ref.shape[0]
    s_len = far_len
    sk = s_len + 2 * tq
    i = i0 + pl.program_id(1)

    @pl.when((pl.program_id(0) == 0) & (pl.program_id(1) == 0))
    def _():
        d = tq + lax.broadcasted_iota(i32, (tq, 2 * tq), 0) - lax.broadcasted_iota(i32, (tq, 2 * tq), 1)
        bucket = _t5_bucket(d)
        for h in range(HB):
            acc = jnp.zeros((tq, 2 * tq), f32)
            for bk in range(N_BUCKETS):
                acc = jnp.where(bucket == bk, rb_ref[bk, h], acc)
            toep_ref[h] = acc

    parts = ([kfar_ref[0:far_len, :]] if far_len else []) + [kn0_ref[...], kn1_ref[...]]
    kall = jnp.concatenate(parts, axis=0)
    kk = kall[:, 0:HD_B]
    vv = kall[:, HD_B:2 * HD_B]
    kidx = kall[:, 2 * HD_B:2 * HD_B + D_IDX]
    col = lax.broadcasted_iota(i32, (tq, sk), 1)
    qpos = i * tq + lax.broadcasted_iota(i32, (tq, sk), 0)
    far = col < s_len
    kpos = jnp.where(far, col, (i - 1) * tq + (col - s_len))
    valid = (far & (col < (i - 1) * tq)) | ((col >= s_len) & (kpos >= 0) & (kpos <= qpos))

    wi = wi_ref[...]
    sc = jnp.zeros((tq, sk), f32)
    for h in range(H_IDX):
        r = _dot_nt(qi_ref[:, h * D_IDX:(h + 1) * D_IDX], kidx)
        sc = sc + jnp.maximum(r, 0.0) * wi[:, h:h + 1]
    sc = jnp.where(valid, sc, NEG_INF)
    sel = _topk_mask(sc, kpos, valid, topk, max(1, (t_len - 1).bit_length()))
    madd = jnp.where(sel, 0.0, NEG_INF)

    for h in range(HB):
        lg = _dot_nt(qb_ref[:, h * HD_B:(h + 1) * HD_B], kk)
        if far_len:
            lg = jnp.concatenate([lg[:, 0:s_len] + rb_ref[N_BUCKETS - 1, h], lg[:, s_len:sk] + toep_ref[h]], axis=1)
        else:
            lg = lg + toep_ref[h]
        lg = lg + madd
        p = jnp.exp(lg - jnp.max(lg, axis=-1, keepdims=True))
        out = _dot(p.astype(bf16), vv) * (1.0 / jnp.sum(p, axis=-1, keepdims=True))
        ob_ref[:, h * HD_B:(h + 1) * HD_B] = out.astype(bf16)


def _dsa_prompt(rel_bias, qbn, qi, wi, dsab, b, t, tq):
    nq = t // tq
    topk = max(1, min(DSA_TOPK, t // 4))
    kw = dsab.shape[1]
    gs = DSA_GROUP if nq % DSA_GROUP == 0 else 1
    outs = []
    for i0 in range(0, nq, gs):
        far_len = max(0, i0 + gs - 2) * tq
        outs.append(pl.pallas_call(
            functools.partial(_dsa_p_kernel, topk=topk, i0=i0, far_len=far_len), grid=(b, gs),
            in_specs=[pl.BlockSpec(memory_space=pltpu.SMEM),
                      pl.BlockSpec((tq, HB * HD_B), lambda bi, j, i0=i0: (bi * nq + i0 + j, 0)),
                      pl.BlockSpec((tq, H_IDX * D_IDX), lambda bi, j, i0=i0: (bi * nq + i0 + j, 0)),
                      pl.BlockSpec((tq, H_IDX), lambda bi, j, i0=i0: (bi * nq + i0 + j, 0)),
                      pl.BlockSpec((t, kw), lambda bi, j: (bi, 0)),
                      pl.BlockSpec((tq, kw), lambda bi, j, i0=i0: (bi * nq + jnp.maximum(i0 + j - 1, 0), 0)),
                      pl.BlockSpec((tq, kw), lambda bi, j, i0=i0: (bi * nq + i0 + j, 0))],
            out_specs=pl.BlockSpec((None, tq, HB * HD_B), lambda bi, j: (bi, j, 0)),
            out_shape=jax.ShapeDtypeStruct((b, gs * tq, HB * HD_B), bf16),
            scratch_shapes=[pltpu.VMEM((HB, tq, 2 * tq), f32)],
            compiler_params=_cparams(2), name="dsa_prompt")(rel_bias, qbn, qi, wi, dsab, dsab, dsab))
    return jnp.concatenate(outs, axis=1).reshape(b * t, HB * HD_B)


def _dsa_s_kernel(pt_ref, rbt_ref, qb_ref, qi_ref, wi_ref, own_ref, ownt_ref, *rest, pg, past, topk):
    pages = rest[:pg]
    o_ref = rest[pg]
    sc_ref, kvt_ref, bias_ref = rest[pg + 1:]
    c = pl.program_id(1)
    sks = past + PAGE_SIZE
    chunk = pg * PAGE_SIZE
    qi = qi_ref[...]
    wi = wi_ref[...]

    @pl.when((pl.program_id(0) == 0) & (c == 0))
    def _():
        d = past - lax.broadcasted_iota(i32, (1, sks), 1)
        bucket = _t5_bucket(d)
        rbt = rbt_ref[...]
        acc = jnp.zeros((HB, sks), f32)
        for bk in range(N_BUCKETS):
            acc = jnp.where(bucket == bk, rbt[:, bk:bk + 1], acc)
        bias_ref[...] = acc

    pgs = jnp.concatenate([r[...] for r in pages], axis=1).astype(bf16)
    off = pl.multiple_of(c * chunk, chunk)
    kvt_ref[:, pl.ds(off, chunk)] = pgs[0:2 * HD_B, :]
    r_idx = _dot(qi, pgs[2 * HD_B:2 * HD_B + D_IDX, :])
    sc_ref[:, pl.ds(off, chunk)] = jnp.sum(jnp.maximum(r_idx, 0.0) * wi, axis=0, keepdims=True)

    @pl.when(c == pl.num_programs(1) - 1)
    def _():
        lane_kv = lax.broadcasted_iota(i32, (2 * HD_B, PAGE_SIZE), 1)
        own_col = jnp.broadcast_to(ownt_ref[0:2 * HD_B, :].astype(f32), (2 * HD_B, PAGE_SIZE))
        kvt_ref[:, past:sks] = jnp.where(lane_kv == 0, own_col, 0.0).astype(bf16)
        own = own_ref[...]
        lane = lax.broadcasted_iota(i32, (1, PAGE_SIZE), 1)
        r_own = jnp.sum(qi.astype(f32) * own[:, 2 * HD_B:2 * HD_B + D_IDX].astype(f32), axis=-1, keepdims=True)
        sc_own = jnp.sum(jnp.maximum(r_own, 0.0) * wi, axis=0, keepdims=True)
        sc_ref[:, past:sks] = jnp.where(lane == 0, sc_own, NEG_INF)
        kpos = lax.broadcasted_iota(i32, (1, sks), 1)
        valid = kpos <= past
        sc = jnp.where(valid, sc_ref[...], NEG_INF)
        sel = _topk_mask_row(sc, kpos, valid, topk, max(1, past.bit_length()))
        madd = jnp.where(sel, 0.0, NEG_INF)
        kvt = kvt_ref[...]
        lg = _dot(qb_ref[...], kvt[0:HD_B, :]) + bias_ref[...] + madd
        p = jnp.exp(lg - jnp.max(lg, axis=-1, keepdims=True))
        p = (p * (1.0 / jnp.sum(p, axis=-1, keepdims=True))).astype(bf16)
        o_ref[...] = _dot_nt(p, kvt[HD_B:2 * HD_B, :]).astype(bf16)


def _dsa_sample(page_table, cache, layer, rbt, qb_s, qi_s, wi_s, own, ownt):
    nb, n_pages = page_table.shape
    pg = PAGES_PER_STEP
    past = n_pages * PAGE_SIZE
    sks = past + PAGE_SIZE
    topk = max(1, min(DSA_TOPK, (past + 1) // 4))
    row_w = cache.shape[2]
    in_specs = [pl.BlockSpec(rbt.shape, lambda bi, c, pt: (0, 0)),
                pl.BlockSpec((None, HB, HD_B), lambda bi, c, pt: (bi, 0, 0)),
                pl.BlockSpec((None, H_IDX, D_IDX), lambda bi, c, pt: (bi, 0, 0)),
                pl.BlockSpec((None, H_IDX, 1), lambda bi, c, pt: (bi, 0, 0)),
                pl.BlockSpec((None, 1, row_w), lambda bi, c, pt: (bi, 0, 0)),
                pl.BlockSpec((None, row_w, 1), lambda bi, c, pt: (bi, 0, 0))]
    in_specs += [pl.BlockSpec((None, None, row_w, PAGE_SIZE),
                              functools.partial(lambda bi, c, pt, kk: (layer, pt[bi, c * pg + kk], 0, 0), kk=kk))
                 for kk in range(pg)]
    grid_spec = pltpu.PrefetchScalarGridSpec(
        num_scalar_prefetch=1, grid=(nb, n_pages // pg), in_specs=in_specs,
        out_specs=pl.BlockSpec((None, HB, HD_B), lambda bi, c, pt: (bi, 0, 0)),
        scratch_shapes=[pltpu.VMEM((1, sks), f32), pltpu.VMEM((2 * HD_B, sks), bf16), pltpu.VMEM((HB, sks), f32)])
    out = pl.pallas_call(
        functools.partial(_dsa_s_kernel, pg=pg, past=past, topk=topk), grid_spec=grid_spec,
        out_shape=jax.ShapeDtypeStruct((nb, HB, HD_B), bf16), compiler_params=_cparams(2), name="dsa_sample")(
        page_table, rbt, qb_s, qi_s, wi_s, own, ownt, *([cache] * pg))
    return out.reshape(nb, HB * HD_B)


def _merge_kernel(oa_ref, ob_ref, wa_ref, wb_ref, ga_ref, gb_ref, o_ref):
    ga = 1.0 / (1.0 + jnp.exp(-ga_ref[...]))
    gb = 1.0 / (1.0 + jnp.exp(-gb_ref[...]))
    o_ref[...] = (ga * _dot(oa_ref[...], wa_ref[...]) + gb * _dot(ob_ref[...], wb_ref[...])).astype(bf16)


def _merge(oa, ob, wa, wb, zbig, d, tm, tn):
    n = oa.shape[0]
    nj = d // tn
    return pl.pallas_call(
        _merge_kernel, grid=(n // tm, nj),
        in_specs=[pl.BlockSpec((tm, oa.shape[1]), lambda i, j: (i, 0)),
                  pl.BlockSpec((tm, ob.shape[1]), lambda i, j: (i, 0)),
                  pl.BlockSpec((wa.shape[0], tn), lambda i, j: (0, j)),
                  pl.BlockSpec((wb.shape[0], tn), lambda i, j: (0, j)),
                  pl.BlockSpec((tm, tn), lambda i, j: (i, j)),
                  pl.BlockSpec((tm, tn), lambda i, j: (i, nj + j))],
        out_specs=pl.BlockSpec((tm, tn), lambda i, j: (i, j)),
        out_shape=jax.ShapeDtypeStruct((n, d), bf16), compiler_params=_cparams(2), name="merge")(
        oa, ob, wa, wb, zbig, zbig)


def _top_rows(s, k):
    r = s.shape[0]
    ridx = lax.broadcasted_iota(i32, s.shape, 0).astype(f32)
    vals, idxs = [], []
    for _ in range(k):
        m = jnp.max(s, axis=0, keepdims=True)
        am = jnp.min(jnp.where(s == m, ridx, float(r)), axis=0, keepdims=True)
        vals.append(m)
        idxs.append(am)
        s = jnp.where(ridx == am, NEG_INF, s)
    return jnp.concatenate(vals, axis=0), jnp.concatenate(idxs, axis=0)


def _pick_rows(tab, sel):
    out = jnp.zeros(sel.shape, tab.dtype)
    for i in range(tab.shape[0]):
        out = jnp.where(sel == i, tab[i:i + 1, :], out)
    return out


def _peer_sel_kernel(q_ref, k1_ref, k2_ref, a_ref, b_ref, g_ref):
    half = PEER_QDIM // 2
    shift = PEER_TOPK.bit_length() - 1
    assert 1 << shift == PEER_TOPK
    k1 = k1_ref[...]
    k2 = k2_ref[...]
    a_rows, b_rows, g_rows = [], [], []
    for h in range(PEER_HEADS):
        q1 = q_ref[:, h * PEER_QDIM:h * PEER_QDIM + half]
        q2 = q_ref[:, h * PEER_QDIM + half:(h + 1) * PEER_QDIM]
        v1, i1 = _top_rows(_dot_nt(k1, q1), PEER_TOPK)
        v2, i2 = _top_rows(_dot_nt(k2, q2), PEER_TOPK)
        cand = jnp.concatenate([v1[i:i + 1, :] + v2 for i in range(PEER_TOPK)], axis=0)
        best, pos = _top_rows(cand, PEER_TOPK)
        pos = pos.astype(i32)
        a_rows.append(_pick_rows(i1, pos >> shift))
        b_rows.append(_pick_rows(i2, pos & (PEER_TOPK - 1)))
        e = jnp.exp(best - best[0:1, :])
        g_rows.append(e / jnp.sum(e, axis=0, keepdims=True))
    a_ref[...] = jnp.concatenate(a_rows, axis=0).T.astype(i32)
    b_ref[...] = jnp.concatenate(b_rows, axis=0).T.astype(i32)
    g_ref[...] = jnp.concatenate(g_rows, axis=0).T


def _peer_select(q, k1, k2, ts):
    n = q.shape[0]
    slots = PEER_HEADS * PEER_TOPK
    row = pl.BlockSpec((ts, slots), lambda i: (i, 0))
    return pl.pallas_call(
        _peer_sel_kernel, grid=(n // ts,),
        in_specs=[pl.BlockSpec((ts, q.shape[1]), lambda i: (i, 0)),
                  pl.BlockSpec(k1.shape, lambda i: (0, 0)), pl.BlockSpec(k2.shape, lambda i: (0, 0))],
        out_specs=[row, row, row],
        out_shape=[jax.ShapeDtypeStruct((n, slots), i32), jax.ShapeDtypeStruct((n, slots), i32),
                   jax.ShapeDtypeStruct((n, slots), f32)],
        compiler_params=_cparams(1), name="peer_select")(q, k1, k2)


def _peer_gate_kernel(a_ref, b_ref, g_ref, o_ref, grid_ref):
    ts, slots = a_ref.shape
    krow = lax.broadcasted_iota(i32, (N_KEYS, slots), 0)

    def group(gi, carry):
        n0 = pl.multiple_of(gi * ROW_ALIGN, ROW_ALIGN)
        for j in range(ROW_ALIGN):
            a = jnp.broadcast_to(a_ref[pl.ds(n0 + j, 1), :], (N_KEYS, slots))
            b = jnp.broadcast_to(b_ref[pl.ds(n0 + j, 1), :], (N_KEYS, slots))
            g = jnp.broadcast_to(g_ref[pl.ds(n0 + j, 1), :], (N_KEYS, slots))
            ma = jnp.where(a == krow, g, 0.0).astype(bf16)
            mb = jnp.where(b == krow, 1.0, 0.0).astype(bf16)
            grid_ref[j] = _dot_nt(ma, mb)
        for k1 in range(N_KEYS):
            o_ref[pl.ds(n0, ROW_ALIGN), k1 * N_KEYS:(k1 + 1) * N_KEYS] = grid_ref[:, k1, :].astype(bf16)
        return carry

    lax.fori_loop(0, ts // ROW_ALIGN, group, 0)


def _peer_gates(a, b, g, ts):
    n, slots = a.shape
    row = pl.BlockSpec((ts, slots), lambda i: (i, 0))
    return pl.pallas_call(
        _peer_gate_kernel, grid=(n // ts,), in_specs=[row, row, row],
        out_specs=pl.BlockSpec((ts, N_KEYS * N_KEYS), lambda i: (i, 0)),
        out_shape=jax.ShapeDtypeStruct((n, N_KEYS * N_KEYS), bf16),
        scratch_shapes=[pltpu.VMEM((ROW_ALIGN, N_KEYS, N_KEYS), f32)],
        compiler_params=_cparams(1), name="peer_gates")(a, b, g)


def _peer_dense_kernel(xn_ref, gd_ref, u_ref, v_ref, o_ref):
    @pl.when(pl.program_id(1) == 0)
    def _():
        o_ref[...] = jnp.zeros_like(o_ref)

    hid = _dot_nt(xn_ref[...], u_ref[...])
    act = gd_ref[...].astype(f32) * (0.5 * hid * (1.0 + lax.erf(hid * (2.0 ** -0.5))))
    o_ref[...] += _dot(act.astype(bf16), v_ref[...])


def _peer_dense(xn, gd, u, v, tm, te):
    n, d = xn.shape
    ne = u.shape[0]
    return pl.pallas_call(
        _peer_dense_kernel, grid=(n // tm, ne // te),
        in_specs=[pl.BlockSpec((tm, d), lambda i, e: (i, 0)), pl.BlockSpec((tm, te), lambda i, e: (i, e)),
                  pl.BlockSpec((te, d), lambda i, e: (e, 0)), pl.BlockSpec((te, d), lambda i, e: (e, 0))],
        out_specs=pl.BlockSpec((tm, d), lambda i, e: (i, 0)),
        out_shape=jax.ShapeDtypeStruct((n, d), f32), compiler_params=_cparams(2), name="peer_dense")(xn, gd, u, v)


def _prep_layer(l, norm1_g, w_in, cq_norm_g, ckv_norm_g, w_uq, w_uk, w_uv, q_norm_a, k_norm_a,
                q_norm_b, k_norm_b, w_up_a, w_up_b, w_out, norm2_g, peer_wq, peer_k1, peer_k2, peer_u, peer_v):
    d = w_in.shape[1]
    win = w_in[l]
    splits = (Q_LORA, KV_LORA, D_ROPE, HB * HD_B, HD_B, HD_B, H_IDX * D_IDX, D_IDX, H_IDX, d, d)
    offs = [0]
    for s in splits:
        offs.append(offs[-1] + s)
    cq, ckv, kpe, qb, kb, vb, qi, ki, wi, ga, gb = [win[:, offs[i]:offs[i + 1]] for i in range(len(splits))]

    def zeros(w):
        return jnp.zeros((d, w), win.dtype)

    small = [cq, ckv, kpe, zeros(LANES - D_ROPE), kb, vb, ki, wi, zeros(LANES - D_IDX - H_IDX)]
    zs_w = sum(a.shape[1] for a in small)
    pad = (-zs_w) % 512
    w_s = jnp.concatenate(small + ([zeros(pad)] if pad else []), axis=1).astype(bf16)
    w_big = jnp.concatenate([ga, gb, qb], axis=1).astype(bf16)
    wuq3 = w_uq[l].reshape(Q_LORA, HA, D_QK_A)
    wuq = jnp.concatenate([wuq3[:, :, :D_NOPE].reshape(Q_LORA, HA * D_NOPE),
                           wuq3[:, :, D_NOPE:].reshape(Q_LORA, HA * D_ROPE)], axis=1).astype(bf16)
    ones_half = jnp.ones((LANES - HD_B,), f32)
    return dict(
        norm1_g=norm1_g[l], norm2_g=norm2_g[l], w_s=w_s, w_big=w_big, w_qi=qi.astype(bf16),
        wuq=wuq, wukt=jnp.transpose(w_uk[l], (1, 2, 0)).astype(bf16),
        wuk2=w_uk[l].reshape(KV_LORA, HA * D_NOPE).astype(bf16),
        wuv=jnp.transpose(w_uv[l], (1, 0, 2)).astype(bf16), wuvf=w_uv[l].reshape(KV_LORA, HA * D_VA).astype(bf16),
        g_cq=cq_norm_g[l].reshape(1, Q_LORA), g_ckv=ckv_norm_g[l].reshape(1, KV_LORA),
        g_qan=q_norm_a[l][:D_NOPE].reshape(1, D_NOPE), g_qap=jnp.tile(q_norm_a[l][D_NOPE:], 2).reshape(1, LANES),
        g_kan=k_norm_a[l][:D_NOPE].reshape(1, D_NOPE),
        g_kap=jnp.concatenate([k_norm_a[l][D_NOPE:], jnp.zeros((LANES - D_ROPE,), f32)]).reshape(1, LANES),
        g_qb=jnp.tile(q_norm_b[l], LANES // HD_B).reshape(1, LANES),
        g_kb=jnp.concatenate([k_norm_b[l], ones_half]).reshape(1, LANES),
        w_up_a=w_up_a[l].astype(bf16), w_up_b=w_up_b[l].astype(bf16), w_out=w_out[l].astype(bf16),
        peer_wq=peer_wq[l].astype(bf16), peer_k1=peer_k1[l].astype(bf16), peer_k2=peer_k2[l].astype(bf16),
        peer_u=peer_u[l].astype(bf16), peer_v=peer_v[l].astype(bf16))


def _rope_tables(pos):
    half = D_ROPE // 2
    inv_freq = ROPE_THETA ** (-jnp.arange(half, dtype=f32) / half)
    ang = pos.astype(f32)[:, None] * inv_freq[None, :]
    cos, sin = jnp.cos(ang), jnp.sin(ang)
    return jnp.tile(jnp.concatenate([cos, cos], axis=1), (1, 2)), jnp.tile(jnp.concatenate([-sin, sin], axis=1), (1, 2))


def kernel(x_prompt, x_sample, cache_mla, cache_mla_kinv, cache_dsa, page_table, rel_bias, norm1_g, w_in, cq_norm_g, ckv_norm_g, w_uq, w_uk, w_uv, q_norm_a, k_norm_a, q_norm_b, k_norm_b, w_up_a, w_up_b, w_out, norm2_g, peer_wq, peer_k1, peer_k2, peer_u, peer_v):
    b, t, d = x_prompt.shape
    nb, ts_, _ = x_sample.shape
    depth = w_in.shape[0]
    assert ts_ == 1 and HA % 2 == 0 and Q_LORA % LANES == 0 and KV_LORA % LANES == 0
    assert cache_mla.shape[2] == PAGE_SIZE and N_KEYS == LANES and PEER_HEADS * PEER_TOPK == LANES
    npr = b * t
    n = npr + nb
    past = page_table.shape[1] * PAGE_SIZE

    tm_big = _row_tile(n, 832)
    tm_feat = _row_tile(n, 208)
    tm_small = _row_tile(n, 208)
    tm_ffn = _row_tile(n, 640)
    tq_a = _row_tile(t, 256)
    tq_b = LANES
    assert t % tq_b == 0 and n % LANES == 0

    pos = jnp.concatenate([jnp.tile(jnp.arange(t, dtype=i32), b), jnp.full((nb,), past, i32)])
    cosq, sinq = _rope_tables(pos)
    rbt = rel_bias.T
    cache_mla_t = jnp.swapaxes(cache_mla, 2, 3)
    cache_kinv_t = jnp.swapaxes(cache_mla_kinv, 2, 3)
    cache_dsa_t = jnp.swapaxes(cache_dsa, 2, 3)

    h_prev = jnp.concatenate([x_prompt.reshape(npr, d), x_sample.reshape(nb, d)], axis=0)
    ffn_prev = None
    rows = {k: [] for k in ("mla", "kinv", "dsa")}
    for l in range(depth):
        wl = _prep_layer(l, norm1_g, w_in, cq_norm_g, ckv_norm_g, w_uq, w_uk, w_uv, q_norm_a, k_norm_a,
                         q_norm_b, k_norm_b, w_up_a, w_up_b, w_out, norm2_g, peer_wq, peer_k1, peer_k2, peer_u, peer_v)
        x, xn = _addnorm(h_prev, ffn_prev, wl["norm1_g"], tm_small)
        if x is None:
            x = h_prev
        zs = _mm(xn, wl["w_s"], f32, tm_big, _col_tile(wl["w_s"].shape[1], 512))
        zbig = _mm(xn, wl["w_big"], f32, tm_big, _col_tile(math.gcd(d, wl["w_big"].shape[1]), 512))
        qi = _mm(xn, wl["w_qi"], bf16, tm_big, _col_tile(wl["w_qi"].shape[1], 512))
        qlat, qpe, mla_row, mlab, kinv, qbn, dsa_row, dsab, wi = _features(zs, zbig, cosq, sinq, wl, tm_feat)
        rows["mla"].append(mla_row)
        rows["kinv"].append(kinv)
        rows["dsa"].append(dsa_row)

        oa_p = _mla_prompt(qlat, qpe, mlab, kinv.T, wl["wuv"], b, t, tq_a)
        oa_s = _mla_sample(page_table, cache_mla_t, cache_kinv_t, l,
                           jnp.transpose(qlat[:, npr:, :], (1, 0, 2)), qpe[npr:].reshape(nb, HA, D_ROPE),
                           mlab[npr:].reshape(nb, 1, KV_LORA + D_ROPE), kinv[npr:].reshape(nb, HA, 1), wl["wuvf"])
        ob_p = _dsa_prompt(rel_bias, qbn, qi, wi, dsab, b, t, tq_b)
        ob_s = _dsa_sample(page_table, cache_dsa_t, l, rbt, qbn[npr:].reshape(nb, HB, HD_B),
                           qi[npr:].reshape(nb, H_IDX, D_IDX), wi[npr:].reshape(nb, H_IDX, 1),
                           dsab[npr:].reshape(nb, 1, 2 * HD_B + D_IDX), dsab[npr:].reshape(nb, 2 * HD_B + D_IDX, 1))
        oa = jnp.concatenate([oa_p, oa_s], axis=0)
        ob = jnp.concatenate([ob_p, ob_s], axis=0)
        tn_d = _col_tile(d, 512)
        mix = _merge(oa, ob, wl["w_up_a"], wl["w_up_b"], zbig, d, tm_big, tn_d)
        h = _mm(mix, wl["w_out"], f32, tm_big, tn_d, res=x)

        _, xn2 = _addnorm(h, None, wl["norm2_g"], tm_small)
        pq = _mm(xn2, wl["peer_wq"], bf16, tm_big, _col_tile(wl["peer_wq"].shape[1], 512))
        ea, eb, eg = _peer_select(pq, wl["peer_k1"], wl["peer_k2"], LANES)
        gd = _peer_gates(ea, eb, eg, LANES)
        ffn_prev = _peer_dense(xn2, gd, wl["peer_u"], wl["peer_v"], tm_ffn, 512)
        h_prev = h

    y = _add(h_prev, ffn_prev, tm_small)

    def stack(key, lo, hi, shape):
        return jnp.stack([r[lo:hi].reshape(shape + (r.shape[1],)) for r in rows[key]])

    return (y[:npr].reshape(b, t, d), y[npr:].reshape(nb, 1, d),
            stack("mla", 0, npr, (b, t)), stack("kinv", 0, npr, (b, t)), stack("dsa", 0, npr, (b, t)),
            stack("mla", npr, n, (nb, 1)), stack("kinv", npr, n, (nb, 1)), stack("dsa", npr, n, (nb, 1)))
```
